```python
import math
import jax
import jax.numpy as jnp
from jax import lax
import numpy as np

D_MODEL = 2048
BATCH = 16
SEQ = 2048
DEPTH = 4

GRID_W = 64
CTX_LEN = 256
N_MIXERS = 3
N_MOD = 6
RMS_EPS = 1e-6

HG_DK = 128
HG_HEADS = D_MODEL // HG_DK
HG_WIDTH = HG_HEADS * HG_DK
HG_CHUNK = 16

GDN_DK = 128
GDN_DV = 128
GDN_QK_HEADS = D_MODEL // GDN_DK
GDN_V_HEADS = 2 * GDN_QK_HEADS
GDN_QK_WIDTH = GDN_QK_HEADS * GDN_DK
GDN_V_WIDTH = GDN_V_HEADS * GDN_DV
GDN_CONV_DIM = 2 * GDN_QK_WIDTH + GDN_V_WIDTH
GDN_IN_WIDTH = GDN_CONV_DIM + GDN_V_WIDTH + 4 * GDN_V_HEADS
GDN_CONV = 5
GDN_CHUNK = 64

NA_HEAD_DIM = 32
NA_HEADS = D_MODEL // NA_HEAD_DIM
NA_WIN_R = 8
NA_WIN_C = 16
NA_QB = 16
NA_KB = NA_QB + NA_WIN_C

D_FF = ((8 * D_MODEL // 3 + 127) // 128) * 128
FFN_CONV = 3

kernel_name = 'hybrid_flow_backbone_hgrn2_gdn_natten'


def _n_layers_of(m):
    return (DEPTH - m + N_MIXERS - 1) // N_MIXERS


def rmsnorm(x, g):
    xf = x.astype(jnp.float32)
    y = xf * lax.rsqrt(jnp.mean(xf * xf, axis=-1, keepdims=True) + RMS_EPS)
    return (y * g.astype(jnp.float32)).astype(x.dtype)


def l2norm(x):
    return x * lax.rsqrt(jnp.sum(x * x, axis=-1, keepdims=True) + RMS_EPS)


def modulate(x, g, shift, scale):
    return rmsnorm(x, g) * (1.0 + scale) + shift


def dwconv_centred(u, w):
    K, T = w.shape[0], u.shape[1]
    pad = K // 2
    up = jnp.pad(u, ((0, 0), (pad, pad), (0, 0)))
    out = up[:, 0:T] * w[0]
    for j in range(1, K):
        out = out + up[:, j:j + T] * w[j]
    return out


def _rev(a):
    return jnp.flip(a, axis=1)


def _ident(a):
    return a


def _to_chunks(a, size):
    B, T = a.shape[:2]
    return jnp.moveaxis(a.reshape(B, T // size, size, *a.shape[2:]), 1, 0)


def _from_chunks(a):
    a = jnp.moveaxis(a, 0, 1)
    return a.reshape(a.shape[0], a.shape[1] * a.shape[2], *a.shape[3:])


def conv_ffn(h, w_up, conv_w, w_down):
    u = dwconv_centred(h @ w_up, conv_w)
    a, b = jnp.split(u, 2, axis=-1)
    return (jax.nn.silu(a) * b) @ w_down


def hgrn2_lower_bound(lb_logits, j):
    cs = jnp.cumsum(jax.nn.softmax(lb_logits.astype(jnp.float32), axis=0), axis=0)
    return cs[j] - cs[0]


def hgrn2_scan(q, k, v, logf, s0, need_out):
    tri = jnp.asarray(np.tril(np.ones((HG_CHUNK, HG_CHUNK), bool)))[None, :, :, None, None]

    def step(S, xs):
        qc, kc, vc, gc = xs
        b = jnp.cumsum(gc, axis=1)
        b_end = b[:, -1]
        S_new = jnp.exp(b_end)[..., None] * S + jnp.einsum('bshk,bshv->bhkv', kc * jnp.exp(b_end[:, None] - b), vc)
        if not need_out:
            return S_new, None
        dec = jnp.exp(jnp.where(tri, b[:, :, None] - b[:, None, :], -jnp.inf))
        att = jnp.einsum('bthk,bshk,btshk->bhts', qc, kc, dec)
        o = jnp.einsum('bthk,bhkv->bthv', qc * jnp.exp(b), S) + jnp.einsum('bhts,bshv->bthv', att, vc)
        return S_new, o

    xs = tuple(_to_chunks(a, HG_CHUNK) for a in (q, k, v, logf))
    S, o = lax.scan(step, s0, xs)
    return (_from_chunks(o) if need_out else None), S


def hgrn2_mixer(h, h_c, w_in, lb, norm_g, w_out, need_ctx_out):
    lbh = lb.reshape(2, HG_HEADS, HG_DK)
    log_lb, log_1m_lb = jnp.log(lbh), jnp.log1p(-lbh)

    def project(u):
        B, T, _ = u.shape
        p = (u @ w_in).astype(jnp.float32).reshape(B, T, 5, HG_HEADS, HG_DK)
        q, inp, gate = jax.nn.silu(p[:, :, 0]), p[:, :, 1], p[:, :, 4]
        keys = [(1.0 - lbh[d]) * jax.nn.sigmoid(-p[:, :, 2 + d]) for d in range(2)]
        logf = [jnp.logaddexp(log_lb[d], log_1m_lb[d] + jax.nn.log_sigmoid(p[:, :, 2 + d])) for d in range(2)]
        return q, inp, keys, logf, gate

    def readout(o, gate):
        B, T = o.shape[:2]
        y = rmsnorm(o, norm_g) * jax.nn.silu(gate)
        return y.reshape(B, T, HG_WIDTH).astype(h.dtype) @ w_out

    q_c, i_c, k_c, f_c, g_c = project(h_c)
    q, i, k, f, g = project(h)
    s0 = jnp.zeros((h.shape[0], HG_HEADS, HG_DK, HG_DK), jnp.float32)
    o, o_c = 0.0, 0.0
    for d in range(2):
        rev = _rev if d == 1 else _ident
        oc_d, s_ctx = hgrn2_scan(rev(q_c), rev(k_c[d]), rev(i_c), rev(f_c[d]), s0, need_ctx_out)
        ol_d, _ = hgrn2_scan(rev(q), rev(k[d]), rev(i), rev(f[d]), s_ctx, True)
        o = o + rev(ol_d)
        if need_ctx_out:
            o_c = o_c + rev(oc_d)
    y_c = readout(o_c, g_c) if need_ctx_out else None
    return readout(o, g), y_c


def gdn_scan(q, k, v, g, beta, s0, need_out):
    B, T, H, _ = k.shape
    C = GDN_CHUNK

    def chunks(a):
        return jnp.moveaxis(_to_chunks(a, C), 3, 2)

    q_c, k_c, v_c, g_c, b_c = (chunks(a) for a in (q, k, v, g, beta))
    g_c = jnp.cumsum(g_c, axis=-1)
    diff = g_c[..., :, None] - g_c[..., None, :]
    strict = np.tril(np.ones((C, C), bool), -1)
    incl = np.tril(np.ones((C, C), bool))
    k_beta = k_c * b_c[..., None]
    lmat = jnp.einsum('nbhtk,nbhsk->nbhts', k_beta, k_c) * jnp.exp(jnp.where(strict, diff, -jnp.inf))
    rhs = jnp.concatenate([v_c * b_c[..., None], k_beta * jnp.exp(g_c)[..., None]], axis=-1)
    sol = lax.linalg.triangular_solve(lmat + jnp.eye(C, dtype=lmat.dtype), rhs,
                                      left_side=True, lower=True, unit_diagonal=True)
    u, w = sol[..., :GDN_DV], sol[..., GDN_DV:]
    g_end = g_c[..., -1]
    k_dec = k_c * jnp.exp(g_end[..., None] - g_c)[..., None]
    xs = (u, w, k_dec, g_end)
    if need_out:
        qk = jnp.einsum('nbhtk,nbhsk->nbhts', q_c, k_c) * jnp.exp(jnp.where(incl, diff, -jnp.inf))
        xs = xs + (q_c * jnp.exp(g_c)[..., None], qk)

    def step(S, xs_i):
        u_i, w_i, kd_i, ge_i = xs_i[:4]
        v_new = u_i - jnp.einsum('bhck,bhkv->bhcv', w_i, S)
        S_new = S * jnp.exp(ge_i)[..., None, None] + jnp.einsum('bhck,bhcv->bhkv', kd_i, v_new)
        if not need_out:
            return S_new, None
        qd_i, qk_i = xs_i[4:]
        o = jnp.einsum('bhck,bhkv->bhcv', qd_i, S) + jnp.einsum('bhts,bhsv->bhtv', qk_i, v_new)
        return S_new, o

    S, o = lax.scan(step, s0, xs)
    if need_out:
        o = jnp.transpose(o, (1, 0, 3, 2, 4)).reshape(B, T, H, GDN_DV)
    return o, S


def gdn_mixer(h, h_c, w_in, conv_w, a_log, dt_bias, norm_g, w_out, need_ctx_out):
    rep = GDN_V_HEADS // GDN_QK_HEADS
    decay_rate = jnp.exp(a_log.astype(jnp.float32))
    dtb = dt_bias.astype(jnp.float32)

    def project(u):
        B, T, _ = u.shape
        p = u @ w_in
        qkv = jax.nn.silu(dwconv_centred(p[..., :GDN_CONV_DIM], conv_w)).astype(jnp.float32)
        q = l2norm(qkv[..., :GDN_QK_WIDTH].reshape(B, T, GDN_QK_HEADS, GDN_DK))
        k = l2norm(qkv[..., GDN_QK_WIDTH:2 * GDN_QK_WIDTH].reshape(B, T, GDN_QK_HEADS, GDN_DK))
        v = qkv[..., 2 * GDN_QK_WIDTH:].reshape(B, T, GDN_V_HEADS, GDN_DV)
        q = jnp.repeat(q, rep, axis=2) * (GDN_DK ** -0.5)
        k = jnp.repeat(k, rep, axis=2)
        z = p[..., GDN_CONV_DIM:GDN_CONV_DIM + GDN_V_WIDTH].astype(jnp.float32).reshape(B, T, GDN_V_HEADS, GDN_DV)
        ba = p[..., GDN_CONV_DIM + GDN_V_WIDTH:].astype(jnp.float32).reshape(B, T, 2, 2, GDN_V_HEADS)
        beta = jax.nn.sigmoid(ba[:, :, :, 0])
        g = -decay_rate * jax.nn.softplus(ba[:, :, :, 1] + dtb)
        return q, k, v, z, beta, g

    def readout(o, z):
        B, T = o.shape[:2]
        y = rmsnorm(o, norm_g) * jax.nn.silu(z)
        return y.reshape(B, T, GDN_V_WIDTH).astype(h.dtype) @ w_out

    q_c, k_c, v_c, z_c, b_c, g_c = project(h_c)
    q, k, v, z, b, g = project(h)
    s0 = jnp.zeros((h.shape[0], GDN_V_HEADS, GDN_DK, GDN_DV), jnp.float32)
    o, o_c = 0.0, 0.0
    for d in range(2):
        rev = _rev if d == 1 else _ident
        oc_d, s_ctx = gdn_scan(rev(q_c), rev(k_c), rev(v_c), rev(g_c[:, :, d]), rev(b_c[:, :, d]), s0, need_ctx_out)
        ol_d, _ = gdn_scan(rev(q), rev(k), rev(v), rev(g[:, :, d]), rev(b[:, :, d]), s_ctx, True)
        o = o + rev(ol_d)
        if need_ctx_out:
            o_c = o_c + rev(oc_d)
    y_c = readout(o_c, z_c) if need_ctx_out else None
    return readout(o, z), y_c


def _na_column_tables():
    nb = GRID_W // NA_QB
    cols = np.arange(GRID_W)
    c0 = np.clip(cols - NA_WIN_C // 2, 0, GRID_W - NA_WIN_C).reshape(nb, NA_QB)
    qcol = cols.reshape(nb, NA_QB)
    kc0 = np.clip(np.arange(nb) * NA_QB - NA_WIN_C // 2, 0, GRID_W - NA_KB)
    kcol = kc0[:, None] + np.arange(NA_KB)
    mask = (kcol[:, None, :] >= c0[:, :, None]) & (kcol[:, None, :] < c0[:, :, None] + NA_WIN_C)
    dc_idx = np.clip(kcol[:, None, :] - qcol[:, :, None] + NA_WIN_C - 1, 0, 2 * NA_WIN_C - 2)
    return mask, dc_idx, [int(s) for s in kc0]


def na_mixer(h, h_c, w_qkv, q_norm_g, k_norm_g, rpb, w_out, need_ctx_out):
    B, T, _ = h.shape
    rows = T // GRID_W
    kr = min(NA_WIN_R, rows)
    nb = GRID_W // NA_QB
    n_lat = kr * NA_KB
    scale = NA_HEAD_DIM ** -0.5

    def project(u):
        p = (u @ w_qkv).reshape(u.shape[0], u.shape[1], 3, NA_HEADS, NA_HEAD_DIM)
        q = rmsnorm(p[:, :, 0], q_norm_g)
        k = rmsnorm(p[:, :, 1], k_norm_g)
        return [jnp.swapaxes(a, 1, 2) for a in (q, k, p[:, :, 2])]

    def readout(o):
        return jnp.swapaxes(o, 1, 2).reshape(o.shape[0], o.shape[2], D_MODEL) @ w_out

    q_c, k_c, v_c = project(h_c)
    q, k, v = project(h)

    y_c = None
    if need_ctx_out:
        s = jnp.einsum('bhqd,bhkd->bhqk', q_c, k_c).astype(jnp.float32) * scale
        y_c = readout(jnp.einsum('bhqk,bhkd->bhqd', jax.nn.softmax(s, axis=-1).astype(v_c.dtype), v_c))

    col_mask_np, dc_idx, kc0 = _na_column_tables()
    col_mask = jnp.asarray(np.broadcast_to(col_mask_np[:, :, None, :], (nb, NA_QB, kr, NA_KB)).reshape(nb, NA_QB, n_lat))
    kg = k.reshape(B, NA_HEADS, rows, GRID_W, NA_HEAD_DIM)
    vg = v.reshape(B, NA_HEADS, rows, GRID_W, NA_HEAD_DIM)
    q_rows = jnp.moveaxis(q.reshape(B, NA_HEADS, rows, GRID_W, NA_HEAD_DIM), 2, 0)

    def row_block(args):
        r, q_r = args
        r0 = jnp.clip(r - kr // 2, 0, rows - kr)
        k_rows = lax.dynamic_slice_in_dim(kg, r0, kr, axis=2)
        v_rows = lax.dynamic_slice_in_dim(vg, r0, kr, axis=2)

        def gather_blocks(a):
            return jnp.stack([a[:, :, :, s:s + NA_KB] for s in kc0], axis=2).reshape(B, NA_HEADS, nb, n_lat, NA_HEAD_DIM)

        k_blk, v_blk = gather_blocks(k_rows), gather_blocks(v_rows)
        q_blk = q_r.reshape(B, NA_HEADS, nb, NA_QB, NA_HEAD_DIM)
        dr_idx = r0 + jnp.arange(kr) - r + (NA_WIN_R - 1)
        bias = jnp.take(rpb, dr_idx, axis=1)[:, :, dc_idx]
        bias = jnp.transpose(bias, (0, 2, 3, 1, 4)).reshape(NA_HEADS, nb, NA_QB, n_lat).astype(jnp.float32)
        s_lat = jnp.einsum('bhnqd,bhnkd->bhnqk', q_blk, k_blk).astype(jnp.float32) * scale + bias
        s_lat = jnp.where(col_mask, s_lat, -jnp.inf)
        s_ctx = jnp.einsum('bhnqd,bhkd->bhnqk', q_blk, k_c).astype(jnp.float32) * scale
        p = jax.nn.softmax(jnp.concatenate([s_lat, s_ctx], axis=-1), axis=-1).astype(v.dtype)
        o = (jnp.einsum('bhnqk,bhnkd->bhnqd', p[..., :n_lat], v_blk)
             + jnp.einsum('bhnqk,bhkd->bhnqd', p[..., n_lat:], v_c))
        return o.reshape(B, NA_HEADS, GRID_W, NA_HEAD_DIM)

    o = lax.map(row_block, (jnp.arange(rows), q_rows))
    o = jnp.transpose(o, (1, 2, 0, 3, 4)).reshape(B, NA_HEADS, T, NA_HEAD_DIM)
    return readout(o), y_c


def setup_inputs(seed: int = 0) -> dict:
    key = jax.random.key(seed)
    keys = iter(jax.random.split(key, 32))
    f32 = jnp.float32
    D = D_MODEL
    n_hg, n_gdn, n_na = _n_layers_of(0), _n_layers_of(1), _n_layers_of(2)

    def normal(shape, scale):
        return jax.random.normal(next(keys), shape, f32) * scale

    def gain(shape):
        return 1.0 + normal(shape, 0.05)

    dt = jnp.exp(jax.random.uniform(next(keys), (n_gdn, 2, GDN_V_HEADS), f32, math.log(1e-3), math.log(1e-1)))
    a_log = jnp.log(jax.random.uniform(next(keys), (n_gdn, 2, GDN_V_HEADS), f32, 1.0, 16.0))
    return {
        'x': normal((BATCH, SEQ, D), 1.0),
        'c': normal((BATCH, D), 1.0),
        'ctx': normal((BATCH, CTX_LEN, D), 1.0),
        'c_ctx': normal((D,), 1.0),
        'ada_w': normal((DEPTH, D, N_MOD * D), 0.5 * D ** -0.5),
        'ada_b': normal((DEPTH, N_MOD * D), 0.01),
        'norm_mix_g': gain((DEPTH, D)),
        'norm_ffn_g': gain((DEPTH, D)),
        'hg_w_in': normal((n_hg, D, 5 * HG_WIDTH), D ** -0.5),
        'hg_lb_logits': normal((n_hg, 2, HG_WIDTH), 0.5),
        'hg_norm_g': gain((n_hg, HG_DK)),
        'hg_w_out': normal((n_hg, HG_WIDTH, D), HG_WIDTH ** -0.5),
        'gdn_w_in': normal((n_gdn, D, GDN_IN_WIDTH), D ** -0.5),
        'gdn_conv_w': normal((n_gdn, GDN_CONV, GDN_CONV_DIM), GDN_CONV ** -0.5),
        'gdn_a_log': a_log,
        'gdn_dt_bias': dt + jnp.log(-jnp.expm1(-dt)),
        'gdn_norm_g': gain((n_gdn, GDN_DV)),
        'gdn_w_out': normal((n_gdn, GDN_V_WIDTH, D), GDN_V_WIDTH ** -0.5),
        'na_w_qkv': normal((n_na, D, 3 * D), D ** -0.5),
        'na_q_norm_g': gain((n_na, NA_HEAD_DIM)),
        'na_k_norm_g': gain((n_na, NA_HEAD_DIM)),
        'na_rpb': normal((n_na, NA_HEADS, 2 * NA_WIN_R - 1, 2 * NA_WIN_C - 1), 0.02),
        'na_w_out': normal((n_na, D, D), D ** -0.5),
        'ffn_w_up': normal((DEPTH, D, 2 * D_FF), D ** -0.5),
        'ffn_conv_w': normal((DEPTH, FFN_CONV, 2 * D_FF), FFN_CONV ** -0.5),
        'ffn_w_down': normal((DEPTH, D_FF, D), D_FF ** -0.5),
    }


def reference(x, c, ctx, c_ctx, ada_w, ada_b, norm_mix_g, norm_ffn_g,
              hg_w_in, hg_lb_logits, hg_norm_g, hg_w_out,
              gdn_w_in, gdn_conv_w, gdn_a_log, gdn_dt_bias, gdn_norm_g, gdn_w_out,
              na_w_qkv, na_q_norm_g, na_k_norm_g, na_rpb, na_w_out,
              ffn_w_up, ffn_conv_w, ffn_w_down):
    s_c = jax.nn.silu(c)
    s_cc = jax.nn.silu(c_ctx)
    for i in range(DEPTH):
        m, j = i % N_MIXERS, i // N_MIXERS
        ctx_next = i < DEPTH - 1
        sh_a, sc_a, g_a, sh_f, sc_f, g_f = jnp.split((s_c @ ada_w[i] + ada_b[i])[:, None, :], N_MOD, axis=-1)
        csh_a, csc_a, cg_a, csh_f, csc_f, cg_f = jnp.split(s_cc @ ada_w[i] + ada_b[i], N_MOD, axis=-1)
        h = modulate(x, norm_mix_g[i], sh_a, sc_a)
        h_c = modulate(ctx, norm_mix_g[i], csh_a, csc_a)
        if m == 0:
            y, y_c = hgrn2_mixer(h, h_c, hg_w_in[j], hgrn2_lower_bound(hg_lb_logits, j),
                                 hg_norm_g[j], hg_w_out[j], ctx_next)
        elif m == 1:
            y, y_c = gdn_mixer(h, h_c, gdn_w_in[j], gdn_conv_w[j], gdn_a_log[j], gdn_dt_bias[j],
                               gdn_norm_g[j], gdn_w_out[j], ctx_next)
        else:
            y, y_c = na_mixer(h, h_c, na_w_qkv[j], na_q_norm_g[j], na_k_norm_g[j], na_rpb[j],
                              na_w_out[j], ctx_next)
        x = x + g_a * y
        x = x + g_f * conv_ffn(modulate(x, norm_ffn_g[i], sh_f, sc_f), ffn_w_up[i], ffn_conv_w[i], ffn_w_down[i])
        if ctx_next:
            ctx = ctx + cg_a * y_c
            ctx = ctx + cg_f * conv_ffn(modulate(ctx, norm_ffn_g[i], csh_f, csc_f),
                                        ffn_w_up[i], ffn_conv_w[i], ffn_w_down[i])
    return x
```

```python
import functools

import numpy as np
import jax
import jax.numpy as jnp
from jax import lax
from jax.experimental import pallas as pl
from jax.experimental.pallas import tpu as pltpu

F32 = jnp.float32
BF16 = jnp.bfloat16

RMS_EPS = 1e-6
N_MOD = 6
HEAD_DIM = 128
SCAN_CHUNK = 64
GRID_W = 64
NA_HEAD_DIM = 32
NA_WIN_R = 8
NA_WIN_C = 16
NA_GROUP = HEAD_DIM // NA_HEAD_DIM
GDN_CONV = 5
FFN_CONV = 3
FFN_TILE = 512
HALO = 16
NEG_BIG = -1e30

V7X_VMEM_BYTES = 64 * 1024 * 1024
VMEM_LIMIT_CAP = 56 * 1024 * 1024

NT_DIMS = (((1,), (1,)), ((), ()))
TN_DIMS = (((0,), (0,)), ((), ()))


def _params(sem, est_bytes):
    limit = int(min(VMEM_LIMIT_CAP, max(32 * 1024 * 1024, est_bytes * 5 // 4)))
    return pltpu.CompilerParams(dimension_semantics=sem, vmem_limit_bytes=limit)


def _pick_tile(n, cap, mult):
    best = None
    for t in range(mult, min(n, cap) + 1, mult):
        if n % t == 0:
            best = t
    assert best is not None, (n, cap, mult)
    return best


def _silu(x):
    return x * jax.nn.sigmoid(x)


def _dot(a, b):
    return jnp.dot(a, b, preferred_element_type=F32)


def _dot_nt(a, b):
    return lax.dot_general(a, b, NT_DIMS, preferred_element_type=F32)


def _dot_tn(a, b):
    return lax.dot_general(a, b, TN_DIMS, preferred_element_type=F32)


def _split_bf16(x, n):
    parts = []
    for _ in range(n):
        p = x.astype(BF16)
        parts.append(p)
        x = x - p.astype(F32)
    return parts


def _segment_rows(mod_ref, idx, is_ctx):
    return jnp.where(is_ctx, mod_ref[0, 0, idx:idx + 1, :], mod_ref[0, 1, idx:idx + 1, :])


def _modulate(x, row0, nctx, gain, mod_ref, i_shift, i_scale):
    ms = jnp.mean(x * x, axis=-1, keepdims=True)
    xn = x * lax.rsqrt(ms + RMS_EPS)
    t = row0 + lax.broadcasted_iota(jnp.int32, (x.shape[0], 1), 0)
    is_ctx = t < nctx
    scale = _segment_rows(mod_ref, i_scale, is_ctx)
    shift = _segment_rows(mod_ref, i_shift, is_ctx)
    return xn * (gain * (1.0 + scale)) + shift


def _ada_kernel(c_ref, w_ref, b_ref, o_ref):
    s = _silu(c_ref[...]).astype(BF16)
    o_ref[0] = _dot(s, w_ref[0]) + b_ref[0]


def _ada_params(c, c_ctx, ada_w, ada_b):
    L, D, N = ada_w.shape
    B = c.shape[0]
    rows = -(-(B + 1) // 16) * 16
    cc = jnp.zeros((rows, D), F32).at[:B].set(c).at[B].set(c_ctx)
    tn = _pick_tile(N, 1024, 128)
    est = 2 * (rows * D * 4 + D * tn * 2 + tn * 4 + rows * tn * 4)
    out = pl.pallas_call(
        _ada_kernel,
        grid=(L, N // tn),
        in_specs=[
            pl.BlockSpec((rows, D), lambda l, j: (0, 0)),
            pl.BlockSpec((1, D, tn), lambda l, j: (l, 0, j)),
            pl.BlockSpec((1, 1, tn), lambda l, j: (l, 0, j)),
        ],
        out_specs=pl.BlockSpec((1, rows, tn), lambda l, j: (l, 0, j)),
        out_shape=jax.ShapeDtypeStruct((L, rows, N), F32),
        compiler_params=_params(("arbitrary", "arbitrary"), est),
        name="ada_ln",
    )(cc, ada_w.astype(BF16), ada_b.reshape(L, 1, N))
    lat = out[:, :B].reshape(L, B, 1, N_MOD, D)
    ctx = jnp.broadcast_to(out[:, B].reshape(L, 1, 1, N_MOD, D), (L, B, 1, N_MOD, D))
    return jnp.concatenate([ctx, lat], axis=2)


def _mod_linear_kernel(x_ref, mod_ref, g_ref, w_ref, o_ref, h_ref, *, tm, nctx, i_shift, i_scale):
    @pl.when(pl.program_id(2) == 0)
    def _():
        h = _modulate(x_ref[0], pl.program_id(1) * tm, nctx, g_ref[...], mod_ref, i_shift, i_scale)
        h_ref[...] = h.astype(BF16)

    o_ref[0] = _dot(h_ref[...], w_ref[...]).astype(o_ref.dtype)


def _mod_linear(xs, mod, gain, w, nctx, i_shift, i_scale, out_dtype):
    B, S, D = xs.shape
    N = w.shape[1]
    tm = _pick_tile(S, 768, 16)
    tn = N if N < 128 else _pick_tile(N, 512, 128)
    osz = jnp.dtype(out_dtype).itemsize
    est = 2 * (tm * D * 4 + 2 * N_MOD * D * 4 + D * tn * 2 + tm * tn * osz) + tm * D * 2 + 3 * tm * D * 4
    return pl.pallas_call(
        functools.partial(_mod_linear_kernel, tm=tm, nctx=nctx, i_shift=i_shift, i_scale=i_scale),
        grid=(B, S // tm, N // tn),
        in_specs=[
            pl.BlockSpec((1, tm, D), lambda b, i, j: (b, i, 0)),
            pl.BlockSpec((1, 2, N_MOD, D), lambda b, i, j: (b, 0, 0, 0)),
            pl.BlockSpec((1, D), lambda b, i, j: (0, 0)),
            pl.BlockSpec((D, tn), lambda b, i, j: (0, j)),
        ],
        out_specs=pl.BlockSpec((1, tm, tn), lambda b, i, j: (b, i, j)),
        out_shape=jax.ShapeDtypeStruct((B, S, N), out_dtype),
        scratch_shapes=[pltpu.VMEM((tm, D), BF16)],
        compiler_params=_params(("arbitrary", "arbitrary", "arbitrary"), est),
        name="mod_linear",
    )(xs, mod, gain.reshape(1, D), w)


def _out_linear_kernel(y_ref, w_ref, x_ref, mod_ref, o_ref, *, tm, nctx, i_gate):
    t = pl.program_id(1) * tm + lax.broadcasted_iota(jnp.int32, (tm, 1), 0)
    gate = _segment_rows(mod_ref, i_gate, t < nctx)
    o_ref[0] = x_ref[0] + gate * _dot(y_ref[0], w_ref[...])


def _out_linear(y, w, xs, mod, nctx, i_gate):
    B, S, K = y.shape
    D = w.shape[1]
    tm = _pick_tile(S, 768, 16)
    tn = _pick_tile(D, 512, 128)
    est = 2 * (tm * K * 2 + K * tn * 2 + 2 * tm * tn * 4 + 2 * N_MOD * tn * 4) + 2 * tm * tn * 4
    return pl.pallas_call(
        functools.partial(_out_linear_kernel, tm=tm, nctx=nctx, i_gate=i_gate),
        grid=(B, S // tm, D // tn),
        in_specs=[
            pl.BlockSpec((1, tm, K), lambda b, i, j: (b, i, 0)),
            pl.BlockSpec((K, tn), lambda b, i, j: (0, j)),
            pl.BlockSpec((1, tm, tn), lambda b, i, j: (b, i, j)),
            pl.BlockSpec((1, 2, N_MOD, tn), lambda b, i, j: (b, 0, 0, j)),
        ],
        out_specs=pl.BlockSpec((1, tm, tn), lambda b, i, j: (b, i, j)),
        out_shape=jax.ShapeDtypeStruct((B, S, D), F32),
        compiler_params=_params(("arbitrary", "arbitrary", "arbitrary"), est),
        name="out_linear",
    )(y, w, xs, mod)


def _ffn_kernel(xp_ref, x_ref, xn_ref, mod_ref, g_ref, wa_ref, wb_ref, ca_ref, cb_ref, wd_ref, o_ref, h_ref,
                *, tm, nctx, S):
    i, f = pl.program_id(1), pl.program_id(2)
    row0 = i * tm
    gain = g_ref[...]

    @pl.when(f == 0)
    def _():
        h_ref[0:HALO] = _modulate(xp_ref[0], row0 - HALO, nctx, gain, mod_ref, 3, 4).astype(BF16)
        h_ref[HALO:HALO + tm] = _modulate(x_ref[0], row0, nctx, gain, mod_ref, 3, 4).astype(BF16)
        h_ref[HALO + tm:] = _modulate(xn_ref[0], row0 + tm, nctx, gain, mod_ref, 3, 4).astype(BF16)

    t = row0 + lax.broadcasted_iota(jnp.int32, (tm, 1), 0)
    has_prev = jnp.logical_and(t != 0, t != nctx).astype(F32)
    has_next = jnp.logical_and(t != nctx - 1, t != S - 1).astype(F32)
    rows = tm + 2 * HALO

    def conv(w_ref, c_ref):
        u = _dot(h_ref[...], w_ref[...])
        prev = pltpu.roll(u, 1, 0)[HALO:HALO + tm] * has_prev
        nxt = pltpu.roll(u, rows - 1, 0)[HALO:HALO + tm] * has_next
        return c_ref[0:1, :] * prev + c_ref[1:2, :] * u[HALO:HALO + tm] + c_ref[2:3, :] * nxt

    act = (_silu(conv(wa_ref, ca_ref)) * conv(wb_ref, cb_ref)).astype(BF16)
    y = _dot(act, wd_ref[...])

    @pl.when(f == 0)
    def _():
        o_ref[0] = y

    @pl.when(f > 0)
    def _():
        o_ref[0] += y

    @pl.when(f == pl.num_programs(2) - 1)
    def _():
        gate = _segment_rows(mod_ref, 5, t < nctx)
        o_ref[0] = x_ref[0] + gate * o_ref[0]


def _conv_ffn(xs, mod, gain, w_up, conv_w, w_down, nctx):
    B, S, D = xs.shape
    ffp = w_down.shape[0]
    nf = ffp // FFN_TILE
    tm = _pick_tile(S, 576, HALO)
    nh = tm // HALO
    last = S // HALO - 1
    tf = FFN_TILE
    est = (2 * (tm * D * 4 + 2 * HALO * D * 4 + 2 * N_MOD * D * 4 + 2 * D * tf * 2 + 2 * 8 * tf * 4 + tf * D * 2 + tm * D * 4)
           + (tm + 2 * HALO) * D * 2 + 8 * (tm + 2 * HALO) * tf * 4)
    return pl.pallas_call(
        functools.partial(_ffn_kernel, tm=tm, nctx=nctx, S=S),
        grid=(B, S // tm, nf),
        in_specs=[
            pl.BlockSpec((1, HALO, D), lambda b, i, f: (b, jnp.maximum(i * nh - 1, 0), 0)),
            pl.BlockSpec((1, tm, D), lambda b, i, f: (b, i, 0)),
            pl.BlockSpec((1, HALO, D), lambda b, i, f: (b, jnp.minimum((i + 1) * nh, last), 0)),
            pl.BlockSpec((1, 2, N_MOD, D), lambda b, i, f: (b, 0, 0, 0)),
            pl.BlockSpec((1, D), lambda b, i, f: (0, 0)),
            pl.BlockSpec((D, tf), lambda b, i, f: (0, f)),
            pl.BlockSpec((D, tf), lambda b, i, f: (0, nf + f)),
            pl.BlockSpec((FFN_CONV, tf), lambda b, i, f: (0, f)),
            pl.BlockSpec((FFN_CONV, tf), lambda b, i, f: (0, nf + f)),
            pl.BlockSpec((tf, D), lambda b, i, f: (f, 0)),
        ],
        out_specs=pl.BlockSpec((1, tm, D), lambda b, i, f: (b, i, 0)),
        out_shape=jax.ShapeDtypeStruct((B, S, D), F32),
        scratch_shapes=[pltpu.VMEM((tm + 2 * HALO, D), BF16)],
        compiler_params=_params(("arbitrary", "arbitrary", "arbitrary"), est),
        name="conv_ffn",
    )(xs, xs, xs, mod, gain.reshape(1, D), w_up, w_up, conv_w, conv_w, w_down)


def _ffn_weights(w_up, conv_w, w_down):
    ff = w_down.shape[0]
    ffp = -(-ff // FFN_TILE) * FFN_TILE
    pad = ffp - ff

    def halves(a):
        return jnp.concatenate([jnp.pad(a[:, :ff], ((0, 0), (0, pad))), jnp.pad(a[:, ff:], ((0, 0), (0, pad)))], axis=1)

    return halves(w_up).astype(BF16), halves(conv_w), jnp.pad(w_down, ((0, pad), (0, 0))).astype(BF16)


def _level_tables(C, reverse):
    n_lev = int(np.log2(C))
    t = np.arange(C)
    mats, masks = [], []
    for lev in range(n_lev):
        half = 1 << lev
        parent = t // (2 * half)
        mid = parent * 2 * half + half
        second = (t % (2 * half)) >= half
        a = np.zeros((C, C), np.float32)
        for tt in range(C):
            if second[tt]:
                a[tt, mid[tt]:tt + 1] = 1.0
            else:
                a[tt, tt + 1:mid[tt]] = 1.0
        mats.append(a)
        masks.append(((parent[:, None] == parent[None, :]) & second[:, None] & ~second[None, :]).astype(np.float32))
    mats.append(np.tril(np.ones((C, C), np.float32)))
    mats.append(np.ones((C, C), np.float32))
    mats, masks = np.stack(mats), np.stack(masks)
    if reverse:
        mats, masks = mats[:, ::-1, ::-1], masks[:, ::-1, ::-1]
    return mats.reshape(-1, C), masks


def _chunk_index(i, d, nch, ncc):
    if d == 0:
        return i
    return jnp.where(i < ncc, ncc - 1 - i, nch - 1 - i + ncc)


def _hgrn2_kernel(q_ref, v_ref, f0_ref, f1_ref, gt_ref, lb_ref, ng_ref, cm_ref, mk_ref, y_ref, o_acc, st_ref,
                  *, S, nctx, hb):
    C = SCAN_CHUNK
    nch, ncc = S // C, nctx // C
    n_lev = mk_ref.shape[1]
    for d in range(2):
        f_ref = f0_ref if d == 0 else f1_ref
        st_ref[...] = jnp.zeros(st_ref.shape, F32)
        log_lb, log_1m_lb, one_m_lb = lb_ref[d, 0:1, :], lb_ref[d, 1:2, :], lb_ref[d, 2:3, :]

        def chunk(i, carry, d=d, f_ref=f_ref, log_lb=log_lb, log_1m_lb=log_1m_lb, one_m_lb=one_m_lb):
            r0 = pl.multiple_of(_chunk_index(i, d, nch, ncc) * C, C)
            rows = pl.ds(r0, C)
            pf = f_ref[0, rows, :]
            a = log_lb
            b = log_1m_lb + (jnp.minimum(pf, 0.0) - jnp.log1p(jnp.exp(-jnp.abs(pf))))
            logf = jnp.maximum(a, b) + jnp.log1p(jnp.exp(-jnp.abs(a - b)))
            k = one_m_lb * jax.nn.sigmoid(-pf)
            q = _silu(q_ref[0, rows, :])
            v = v_ref[0, rows, :]
            hi, lo = _split_bf16(logf, 2)
            seg = _dot(cm_ref[d], jnp.concatenate([hi, lo], axis=0))
            for h in range(hb):
                sl = slice(h * HEAD_DIM, (h + 1) * HEAD_DIM)
                qh, kh, vh, sg = q[:, sl], k[:, sl], v[:, sl], seg[:, sl]
                vb = vh.astype(BF16)
                att = jnp.zeros((C, C), F32)
                for lev in range(n_lev):
                    e = jnp.exp(sg[lev * C:(lev + 1) * C])
                    att = att + mk_ref[d, lev] * _dot_nt((qh * e).astype(BF16), (kh * e).astype(BF16))
                cum, tot = sg[n_lev * C:(n_lev + 1) * C], sg[(n_lev + 1) * C:]
                st = st_ref[h]
                o = (_dot(att.astype(BF16), vb) + jnp.sum(qh * kh, axis=-1, keepdims=True) * vh
                     + _dot_nt((qh * jnp.exp(cum)).astype(BF16), st.astype(BF16)))
                st_ref[h] = st * jnp.exp(tot[0:1, :]) + _dot_tn(vb, (kh * jnp.exp(tot - cum)).astype(BF16))
                if d == 0:
                    o_acc[rows, sl] = o
                else:
                    o_acc[rows, sl] += o
            return carry

        lax.fori_loop(0, nch, chunk, 0)

    def readout(i, carry):
        rows = pl.ds(pl.multiple_of(i * C, C), C)
        for h in range(hb):
            sl = slice(h * HEAD_DIM, (h + 1) * HEAD_DIM)
            o = o_acc[rows, sl]
            y = o * lax.rsqrt(jnp.mean(o * o, axis=-1, keepdims=True) + RMS_EPS) * ng_ref[...]
            y_ref[0, rows, sl] = (y * _silu(gt_ref[0, rows, sl])).astype(BF16)
        return carry

    lax.fori_loop(0, nch, readout, 0)


def _hgrn2_scan(p, lb, norm_g, nctx):
    B, S, W5 = p.shape
    W = W5 // 5
    hb = 2 if W % (2 * HEAD_DIM) == 0 else 1
    wb = hb * HEAD_DIM
    nhg = W // wb
    C = SCAN_CHUNK
    tabs = [_level_tables(C, rev) for rev in (False, True)]
    cm = jnp.asarray(np.stack([np.concatenate([t[0], t[0]], axis=1) for t in tabs]), BF16)
    mk = jnp.asarray(np.stack([t[1] for t in tabs]), F32)
    lbp = jnp.stack([jnp.log(lb), jnp.log1p(-lb), 1.0 - lb], axis=1)
    est = 2 * (5 * S * wb * 4 + S * wb * 2) + S * wb * 4 + 16 * C * wb * 4 * 8
    col = lambda k: pl.BlockSpec((1, S, wb), lambda b, g, k=k: (b, 0, k * nhg + g))
    return pl.pallas_call(
        functools.partial(_hgrn2_kernel, S=S, nctx=nctx, hb=hb),
        grid=(B, nhg),
        in_specs=[
            col(0), col(1), col(2), col(3), col(4),
            pl.BlockSpec((2, 3, wb), lambda b, g: (0, 0, g)),
            pl.BlockSpec((1, HEAD_DIM), lambda b, g: (0, 0)),
            pl.BlockSpec(cm.shape, lambda b, g: (0, 0, 0)),
            pl.BlockSpec(mk.shape, lambda b, g: (0, 0, 0, 0)),
        ],
        out_specs=pl.BlockSpec((1, S, wb), lambda b, g: (b, 0, g)),
        out_shape=jax.ShapeDtypeStruct((B, S, W), BF16),
        scratch_shapes=[pltpu.VMEM((S, wb), F32), pltpu.VMEM((hb, HEAD_DIM, HEAD_DIM), F32)],
        compiler_params=_params(("arbitrary", "arbitrary"), est),
        name="hgrn2_scan",
    )(p, p, p, p, p, lbp, norm_g.reshape(1, HEAD_DIM), cm, mk)


GDN_ROWS = 256


def _softplus(x):
    return jnp.maximum(x, 0.0) + jnp.log1p(jnp.exp(-jnp.abs(x)))


def _conv5_silu(x_ref, w_ref, r0, S, nctx):
    n = GDN_ROWS + 16
    lo = pl.multiple_of(jnp.maximum(r0 - 8, 0), 8)
    hi = pl.multiple_of(jnp.minimum(r0 + GDN_ROWS, S - 8), 8)
    ext = jnp.concatenate([x_ref[0, pl.ds(lo, 8), :], x_ref[0, pl.ds(r0, GDN_ROWS), :], x_ref[0, pl.ds(hi, 8), :]], axis=0)
    t = r0 + lax.broadcasted_iota(jnp.int32, (GDN_ROWS, 1), 0)
    pos = jnp.where(t < nctx, t, t - nctx)
    length = jnp.where(t < nctx, nctx, S - nctx)
    half = GDN_CONV // 2
    acc = w_ref[half:half + 1, :] * ext[8:8 + GDN_ROWS]
    for off in range(-half, half + 1):
        if off == 0:
            continue
        valid = jnp.logical_and(pos + off >= 0, pos + off < length).astype(F32)
        tap = pltpu.roll(ext, (-off) % n, 0)[8:8 + GDN_ROWS]
        acc = acc + (w_ref[off + half:off + half + 1, :] * valid) * tap
    return _silu(acc)


def _l2norm(x):
    return x * lax.rsqrt(jnp.sum(x * x, axis=-1, keepdims=True) + RMS_EPS)


def _gdn_kernel(pq_ref, pk_ref, pv_ref, pz_ref, ba_ref, bat_ref, cq_ref, ck_ref, cv_ref, gl_ref, gc_ref, ng_ref,
                cmc_ref, cmr_ref, mk_ref, y_ref, qn_ref, kn_ref, vv_ref, o_acc, st_ref, *, S, nctx, n_vheads):
    C = SCAN_CHUNK
    nch, ncc = S // C, nctx // C
    n_lev = mk_ref.shape[1] - 2
    nba = ba_ref.shape[2]
    hq = pl.program_id(1)

    def prologue(i, carry):
        r0 = pl.multiple_of(i * GDN_ROWS, GDN_ROWS)
        rows = pl.ds(r0, GDN_ROWS)
        qn_ref[rows, :] = _l2norm(_conv5_silu(pq_ref, cq_ref, r0, S, nctx)) * (HEAD_DIM ** -0.5)
        kn_ref[rows, :] = _l2norm(_conv5_silu(pk_ref, ck_ref, r0, S, nctx))
        vv_ref[rows, :] = _conv5_silu(pv_ref, cv_ref, r0, S, nctx)
        return carry

    lax.fori_loop(0, S // GDN_ROWS, prologue, 0)

    lane = lax.broadcasted_iota(jnp.int32, (1, nba), 1)
    sub = lax.broadcasted_iota(jnp.int32, (nba, 1), 0)
    for d in range(2):
        st_ref[...] = jnp.zeros(st_ref.shape, F32)
        strict, incl = mk_ref[d, n_lev], mk_ref[d, n_lev + 1]
        eye = incl - strict

        def chunk(i, carry, d=d, strict=strict, incl=incl, eye=eye):
            r0 = pl.multiple_of(_chunk_index(i, d, nch, ncc) * C, C)
            rows = pl.ds(r0, C)
            ba = ba_ref[0, rows, :]
            beta_all = jax.nn.sigmoid(ba)
            g_all = gl_ref[0:1, :] * _softplus(ba + gl_ref[1:2, :])
            cum_all = _dot(cmc_ref[d], jnp.concatenate(_split_bf16(g_all, 3), axis=0))
            g_rows = gc_ref[:, 0:1] * _softplus(bat_ref[0, _chunk_index(i, d, nch, ncc)] + gc_ref[:, 1:2])
            cum_rows = _dot(jnp.concatenate(_split_bf16(g_rows, 3), axis=1), cmr_ref[d])
            kc, qc = kn_ref[rows, :], qn_ref[rows, :]
            kb, qb = kc.astype(BF16), qc.astype(BF16)
            kk = _dot_nt(kb, kb)
            qk = _dot_nt(qb, kb)
            for h in range(2):
                hv = 2 * hq + h
                sel_b = (lane == d * 2 * n_vheads + hv).astype(F32)
                sel_a = (lane == d * 2 * n_vheads + n_vheads + hv).astype(F32)
                sel_r = (sub == d * 2 * n_vheads + n_vheads + hv).astype(F32)
                beta = jnp.sum(beta_all * sel_b, axis=-1, keepdims=True)
                gcum = jnp.sum(cum_all[:C] * sel_a, axis=-1, keepdims=True)
                gtot = jnp.sum(cum_all[C:] * sel_a, axis=-1, keepdims=True)
                grow = jnp.sum(cum_rows * sel_r, axis=0, keepdims=True)
                diff = gcum - grow
                dec_s = jnp.exp(jnp.where(strict > 0.5, diff, NEG_BIG))
                dec_i = jnp.exp(jnp.where(incl > 0.5, diff, NEG_BIG))
                lmat = beta * kk * dec_s
                tinv = eye - lmat * mk_ref[d, 0]
                for lev in range(1, n_lev):
                    tb = tinv.astype(BF16)
                    off = (lmat * mk_ref[d, lev]).astype(BF16)
                    tinv = tinv - _dot(_dot(tb, off).astype(BF16), tb)
                sl = slice(h * HEAD_DIM, (h + 1) * HEAD_DIM)
                vc = vv_ref[rows, sl]
                rhs = jnp.concatenate([vc * beta, kc * (beta * jnp.exp(gcum))], axis=1).astype(BF16)
                sol = _dot(tinv.astype(BF16), rhs)
                u, w = sol[:, :HEAD_DIM], sol[:, HEAD_DIM:]
                st = st_ref[h]
                sb = st.astype(BF16)
                v_new = u - _dot(w.astype(BF16), sb)
                vnb = v_new.astype(BF16)
                o = _dot((qc * jnp.exp(gcum)).astype(BF16), sb) + _dot((qk * dec_i).astype(BF16), vnb)
                st_ref[h] = st * jnp.exp(gtot[0:1, :]) + _dot_tn((kc * jnp.exp(gtot - gcum)).astype(BF16), vnb)
                if d == 0:
                    o_acc[rows, sl] = o
                else:
                    o_acc[rows, sl] += o
            return carry

        lax.fori_loop(0, nch, chunk, 0)

    def readout(i, carry):
        rows = pl.ds(pl.multiple_of(i * C, C), C)
        for h in range(2):
            sl = slice(h * HEAD_DIM, (h + 1) * HEAD_DIM)
            o = o_acc[rows, sl]
            y = o * lax.rsqrt(jnp.mean(o * o, axis=-1, keepdims=True) + RMS_EPS) * ng_ref[...]
            y_ref[0, rows, sl] = (y * _silu(pz_ref[0, rows, sl])).astype(BF16)
        return carry

    lax.fori_loop(0, nch, readout, 0)


def _gdn_scan(p, ba, conv_w, a_log, dt_bias, norm_g, nctx):
    B, S, W6 = p.shape
    W = W6 // 6
    nq = W // HEAD_DIM
    hv = 2 * nq
    nba = ba.shape[2]
    assert nba == 4 * hv and S % GDN_ROWS == 0 and nctx % GDN_ROWS == 0
    C = SCAN_CHUNK
    n_lev = int(np.log2(C))
    tri, masks = [], []
    for rev in (False, True):
        mats, mk = _level_tables(C, rev)
        tri.append(mats[n_lev * C:])
        strict = mk.sum(0)
        masks.append(np.concatenate([mk, strict[None], (strict + np.eye(C, dtype=np.float32))[None]], axis=0))
    cmc = jnp.asarray(np.stack([np.concatenate([t, t, t], axis=1) for t in tri]), BF16)
    cmr = jnp.asarray(np.stack([np.concatenate([t[:C].T, t[:C].T, t[:C].T], axis=0) for t in tri]), BF16)
    mk = jnp.asarray(np.stack(masks), F32)
    zeros = jnp.zeros((2, hv), F32)
    neg_rate = jnp.stack([zeros, -jnp.exp(a_log.astype(F32))], axis=1).reshape(nba)
    dtb = jnp.stack([zeros, dt_bias.astype(F32)], axis=1).reshape(nba)
    gl = jnp.stack([neg_rate, dtb], axis=0)
    bat = jnp.swapaxes(ba.reshape(B, S // C, C, nba), 2, 3)
    q128, q256 = W // HEAD_DIM, W // (2 * HEAD_DIM)
    est = (2 * (2 * S * 128 * 4 + 2 * S * 256 * 4 + 2 * S * max(nba, 128) * 4 + S * 256 * 2)
           + 2 * S * 128 * 4 + 2 * S * 256 * 4 + 64 * C * 256 * 4)
    return pl.pallas_call(
        functools.partial(_gdn_kernel, S=S, nctx=nctx, n_vheads=hv),
        grid=(B, nq),
        in_specs=[
            pl.BlockSpec((1, S, HEAD_DIM), lambda b, g: (b, 0, g)),
            pl.BlockSpec((1, S, HEAD_DIM), lambda b, g: (b, 0, q128 + g)),
            pl.BlockSpec((1, S, 2 * HEAD_DIM), lambda b, g: (b, 0, 2 * q256 + g)),
            pl.BlockSpec((1, S, 2 * HEAD_DIM), lambda b, g: (b, 0, 4 * q256 + g)),
            pl.BlockSpec((1, S, nba), lambda b, g: (b, 0, 0)),
            pl.BlockSpec((1, S // C, nba, C), lambda b, g: (b, 0, 0, 0)),
            pl.BlockSpec((GDN_CONV, HEAD_DIM), lambda b, g: (0, g)),
            pl.BlockSpec((GDN_CONV, HEAD_DIM), lambda b, g: (0, q128 + g)),
            pl.BlockSpec((GDN_CONV, 2 * HEAD_DIM), lambda b, g: (0, 2 * q256 + g)),
            pl.BlockSpec((2, nba), lambda b, g: (0, 0)),
            pl.BlockSpec((nba, 2), lambda b, g: (0, 0)),
            pl.BlockSpec((1, HEAD_DIM), lambda b, g: (0, 0)),
            pl.BlockSpec(cmc.shape, lambda b, g: (0, 0, 0)),
            pl.BlockSpec(cmr.shape, lambda b, g: (0, 0, 0)),
            pl.BlockSpec(mk.shape, lambda b, g: (0, 0, 0, 0)),
        ],
        out_specs=pl.BlockSpec((1, S, 2 * HEAD_DIM), lambda b, g: (b, 0, g)),
        out_shape=jax.ShapeDtypeStruct((B, S, 2 * W), BF16),
        scratch_shapes=[pltpu.VMEM((S, HEAD_DIM), F32), pltpu.VMEM((S, HEAD_DIM), F32), pltpu.VMEM((S, 2 * HEAD_DIM), F32),
                        pltpu.VMEM((S, 2 * HEAD_DIM), F32), pltpu.VMEM((2, HEAD_DIM, HEAD_DIM), F32)],
        compiler_params=_params(("arbitrary", "arbitrary"), est),
        name="gdn_scan",
    )(p, p, p, p, ba, bat, conv_w, conv_w, conv_w, gl, gl.T, norm_g.reshape(1, HEAD_DIM), cmc, cmr, mk)


NA_ROWS = 256
NA_PAIRS = NA_WIN_R // 2


def _na_bias_table(rpb):
    q = np.arange(GRID_W)[:, None]
    kc = np.arange(GRID_W)[None, :]
    c0 = np.clip(q - NA_WIN_C // 2, 0, GRID_W - NA_WIN_C)
    valid = (kc >= c0) & (kc < c0 + NA_WIN_C)
    idx = np.clip(kc - q + NA_WIN_C - 1, 0, 2 * NA_WIN_C - 2)
    t = jnp.where(valid, rpb.astype(F32)[:, :, idx], NEG_BIG)
    return jnp.concatenate([t[:, :-1], t[:, 1:]], axis=-1)


def _softmax_pv(scores, values):
    m = functools.reduce(jnp.maximum, [jnp.max(s, axis=-1, keepdims=True) for s in scores])
    ps = [jnp.exp(s - m) for s in scores]
    den = functools.reduce(jnp.add, [jnp.sum(p, axis=-1, keepdims=True) for p in ps])
    num = functools.reduce(jnp.add, [_dot(p.astype(BF16), v) for p, v in zip(ps, values)])
    return num / den


def _na_kernel(q_ref, k_ref, v_ref, qg_ref, kg_ref, gm_ref, hm_ref, t2_ref, o_ref, qn_ref, kn_ref, vb_ref,
               *, S, nctx, grid_rows):
    scale = NA_HEAD_DIM ** -0.5

    def head_rms(x):
        hi, lo = _split_bf16(x * x, 2)
        ms = _dot(hi, gm_ref[...]) + _dot(lo, gm_ref[...])
        return x * lax.rsqrt(ms + RMS_EPS)

    def prologue(i, carry):
        rows = pl.ds(pl.multiple_of(i * NA_ROWS, NA_ROWS), NA_ROWS)
        qn_ref[rows, :] = (head_rms(q_ref[0, rows, :]) * (qg_ref[...] * scale)).astype(BF16)
        kn_ref[rows, :] = (head_rms(k_ref[0, rows, :]) * kg_ref[...]).astype(BF16)
        vb_ref[rows, :] = v_ref[0, rows, :].astype(BF16)
        return carry

    lax.fori_loop(0, S // NA_ROWS, prologue, 0)

    def attend(qrow, score_fn, values):
        out = jnp.zeros((GRID_W, HEAD_DIM), F32)
        for j in range(NA_GROUP):
            qj = qrow * hm_ref[j].astype(BF16)
            out = out + _softmax_pv(score_fn(qj, j), values) * hm_ref[j]
        return out.astype(BF16)

    def ctx_block(i, carry):
        rows = pl.ds(pl.multiple_of(i * GRID_W, GRID_W), GRID_W)
        k_ctx, v_ctx = kn_ref[0:nctx, :], vb_ref[0:nctx, :]
        o_ref[0, rows, :] = attend(qn_ref[rows, :], lambda qj, j: [_dot_nt(qj, k_ctx)], [v_ctx])
        return carry

    lax.fori_loop(0, nctx // GRID_W, ctx_block, 0)

    def grid_row(r, carry):
        r0 = jnp.clip(r - NA_WIN_R // 2, 0, grid_rows - NA_WIN_R)
        dr0 = r0 - r + NA_WIN_R - 1
        rows = pl.ds(pl.multiple_of(nctx + r * GRID_W, GRID_W), GRID_W)
        keys = pl.ds(pl.multiple_of(nctx + r0 * GRID_W, GRID_W), NA_WIN_R * GRID_W)
        k_lat, v_lat = kn_ref[keys, :], vb_ref[keys, :]
        k_ctx, v_ctx = kn_ref[0:nctx, :], vb_ref[0:nctx, :]

        def scores(qj, j):
            bias = jnp.concatenate([t2_ref[0, j, dr0 + 2 * m] for m in range(NA_PAIRS)], axis=1)
            return [_dot_nt(qj, k_lat) + bias, _dot_nt(qj, k_ctx)]

        o_ref[0, rows, :] = attend(qn_ref[rows, :], scores, [v_lat, v_ctx])
        return carry

    lax.fori_loop(0, grid_rows, grid_row, 0)


def _na_attention(p, q_gain, k_gain, rpb, nctx):
    B, S, D3 = p.shape
    D = D3 // 3
    grid_rows = (S - nctx) // GRID_W
    assert (S - nctx) % GRID_W == 0 and grid_rows >= NA_WIN_R and S % NA_ROWS == 0 and nctx % GRID_W == 0
    ng = D // HEAD_DIM
    t2 = _na_bias_table(rpb).reshape(ng, NA_GROUP, 2 * NA_WIN_R - 2, GRID_W, 2 * GRID_W)
    lane_head = np.arange(HEAD_DIM) // NA_HEAD_DIM
    gm = jnp.asarray((lane_head[:, None] == lane_head[None, :]) / NA_HEAD_DIM, BF16)
    hm = jnp.asarray((np.arange(NA_GROUP)[:, None, None] == lane_head[None, None, :]), F32)
    tile = lambda g: jnp.tile(g.astype(F32), NA_GROUP).reshape(1, HEAD_DIM)
    est = 2 * (3 * S * 128 * 4 + t2.shape[1] * t2.shape[2] * 64 * 128 * 4 + S * 128 * 2) + 3 * S * 128 * 2 + 64 * 64 * 1024 * 4
    col = lambda k: pl.BlockSpec((1, S, HEAD_DIM), lambda b, g, k=k: (b, 0, k * ng + g))
    return pl.pallas_call(
        functools.partial(_na_kernel, S=S, nctx=nctx, grid_rows=grid_rows),
        grid=(B, ng),
        in_specs=[
            col(0), col(1), col(2),
            pl.BlockSpec((1, HEAD_DIM), lambda b, g: (0, 0)),
            pl.BlockSpec((1, HEAD_DIM), lambda b, g: (0, 0)),
            pl.BlockSpec(gm.shape, lambda b, g: (0, 0)),
            pl.BlockSpec(hm.shape, lambda b, g: (0, 0, 0)),
            pl.BlockSpec((1,) + t2.shape[1:], lambda b, g: (g, 0, 0, 0, 0)),
        ],
        out_specs=pl.BlockSpec((1, S, HEAD_DIM), lambda b, g: (b, 0, g)),
        out_shape=jax.ShapeDtypeStruct((B, S, D), BF16),
        scratch_shapes=[pltpu.VMEM((S, HEAD_DIM), BF16)] * 3,
        compiler_params=_params(("arbitrary", "arbitrary"), est),
        name="na_attention",
    )(p, p, p, tile(q_gain), tile(k_gain), gm, hm, t2)


def _hgrn2_lower_bound(lb_logits, j):
    cs = jnp.cumsum(jax.nn.softmax(lb_logits.astype(F32), axis=0), axis=0)
    return cs[j] - cs[0]


def kernel(x, c, ctx, c_ctx, ada_w, ada_b, norm_mix_g, norm_ffn_g, hg_w_in, hg_lb_logits, hg_norm_g, hg_w_out,
           gdn_w_in, gdn_conv_w, gdn_a_log, gdn_dt_bias, gdn_norm_g, gdn_w_out, na_w_qkv, na_q_norm_g, na_k_norm_g,
           na_rpb, na_w_out, ffn_w_up, ffn_conv_w, ffn_w_down):
    depth = ada_w.shape[0]
    nctx = ctx.shape[1]
    D = x.shape[2]
    xs = jnp.concatenate([ctx, x], axis=1)
    mods = _ada_params(c, c_ctx, ada_w, ada_b)
    for i in range(depth):
        m, j = i % 3, i // 3
        mod = mods[i]
        if m == 0:
            p = _mod_linear(xs, mod, norm_mix_g[i], hg_w_in[j].astype(BF16), nctx, 0, 1, F32)
            y = _hgrn2_scan(p, _hgrn2_lower_bound(hg_lb_logits, j), hg_norm_g[j], nctx)
            w_out = hg_w_out[j]
        elif m == 1:
            w_in = gdn_w_in[j].astype(BF16)
            p = _mod_linear(xs, mod, norm_mix_g[i], w_in[:, :6 * D], nctx, 0, 1, F32)
            ba = _mod_linear(xs, mod, norm_mix_g[i], w_in[:, 6 * D:], nctx, 0, 1, F32)
            y = _gdn_scan(p, ba, gdn_conv_w[j], gdn_a_log[j], gdn_dt_bias[j], gdn_norm_g[j], nctx)
            w_out = gdn_w_out[j]
        else:
            p = _mod_linear(xs, mod, norm_mix_g[i], na_w_qkv[j].astype(BF16), nctx, 0, 1, F32)
            y = _na_attention(p, na_q_norm_g[j], na_k_norm_g[j], na_rpb[j], nctx)
            w_out = na_w_out[j]
        xs = _out_linear(y, w_out.astype(BF16), xs, mod, nctx, 2)
        w_up, conv_w, w_down = _ffn_weights(ffn_w_up[i], ffn_conv_w[i], ffn_w_down[i])
        xs = _conv_ffn(xs, mod, norm_ffn_g[i], w_up, conv_w, w_down, nctx)
    return xs[:, nctx:]
```

```python
import functools

import numpy as np
import jax
import jax.numpy as jnp
from jax import lax
from jax.experimental import pallas as pl
from jax.experimental.pallas import tpu as pltpu

F32 = jnp.float32
BF16 = jnp.bfloat16

RMS_EPS = 1e-6
N_MOD = 6
HEAD_DIM = 128
SCAN_CHUNK = 64
GRID_W = 64
NA_HEAD_DIM = 32
NA_WIN_R = 8
NA_WIN_C = 16
NA_GROUP = HEAD_DIM // NA_HEAD_DIM
GDN_CONV = 5
FFN_CONV = 3
FFN_TILE = 512
HALO = 16
NEG_BIG = -1e30

V7X_VMEM_BYTES = 64 * 1024 * 1024
VMEM_LIMIT_CAP = 56 * 1024 * 1024

NT_DIMS = (((1,), (1,)), ((), ()))
TN_DIMS = (((0,), (0,)), ((), ()))


def _params(sem, est_bytes):
    limit = int(min(VMEM_LIMIT_CAP, max(32 * 1024 * 1024, est_bytes * 5 // 4)))
    return pltpu.CompilerParams(dimension_semantics=sem, vmem_limit_bytes=limit)


def _pick_tile(n, cap, mult):
    best = None
    for t in range(mult, min(n, cap) + 1, mult):
        if n % t == 0:
            best = t
    assert best is not None, (n, cap, mult)
    return best


def _silu(x):
    return x * jax.nn.sigmoid(x)


def _dot(a, b):
    return jnp.dot(a, b, preferred_element_type=F32)


def _dot_nt(a, b):
    return lax.dot_general(a, b, NT_DIMS, preferred_element_type=F32)


def _dot_tn(a, b):
    return lax.dot_general(a, b, TN_DIMS, preferred_element_type=F32)


def _split_bf16(x, n):
    parts = []
    for _ in range(n):
        p = x.astype(BF16)
        parts.append(p)
        x = x - p.astype(F32)
    return parts


def _segment_rows(mod_ref, idx, is_ctx):
    return jnp.where(is_ctx, mod_ref[0, 0, idx:idx + 1, :], mod_ref[0, 1, idx:idx + 1, :])


def _modulate(x, row0, nctx, gain, mod_ref, i_shift, i_scale):
    ms = jnp.mean(x * x, axis=-1, keepdims=True)
    xn = x * lax.rsqrt(ms + RMS_EPS)
    t = row0 + lax.broadcasted_iota(jnp.int32, (x.shape[0], 1), 0)
    is_ctx = t < nctx
    scale = _segment_rows(mod_ref, i_scale, is_ctx)
    shift = _segment_rows(mod_ref, i_shift, is_ctx)
    return xn * (gain * (1.0 + scale)) + shift


def _ada_kernel(c_ref, w_ref, b_ref, o_ref):
    s = _silu(c_ref[...]).astype(BF16)
    o_ref[0] = _dot(s, w_ref[0]) + b_ref[0]


def _ada_params(c, c_ctx, ada_w, ada_b):
    L, D, N = ada_w.shape
    B = c.shape[0]
    rows = -(-(B + 1) // 16) * 16
    cc = jnp.zeros((rows, D), F32).at[:B].set(c).at[B].set(c_ctx)
    tn = _pick_tile(N, 1024, 128)
    est = 2 * (rows * D * 4 + D * tn * 2 + tn * 4 + rows * tn * 4)
    out = pl.pallas_call(
        _ada_kernel,
        grid=(L, N // tn),
        in_specs=[
            pl.BlockSpec((rows, D), lambda l, j: (0, 0)),
            pl.BlockSpec((1, D, tn), lambda l, j: (l, 0, j)),
            pl.BlockSpec((1, 1, tn), lambda l, j: (l, 0, j)),
        ],
        out_specs=pl.BlockSpec((1, rows, tn), lambda l, j: (l, 0, j)),
        out_shape=jax.ShapeDtypeStruct((L, rows, N), F32),
        compiler_params=_params(("arbitrary", "arbitrary"), est),
        name="ada_ln",
    )(cc, ada_w.astype(BF16), ada_b.reshape(L, 1, N))
    lat = out[:, :B].reshape(L, B, 1, N_MOD, D)
    ctx = jnp.broadcast_to(out[:, B].reshape(L, 1, 1, N_MOD, D), (L, B, 1, N_MOD, D))
    return jnp.concatenate([ctx, lat], axis=2)


def _mod_linear_kernel(x_ref, mod_ref, g_ref, w_ref, o_ref, h_ref, *, tm, nctx, i_shift, i_scale):
    @pl.when(pl.program_id(2) == 0)
    def _():
        h = _modulate(x_ref[0], pl.program_id(1) * tm, nctx, g_ref[...], mod_ref, i_shift, i_scale)
        h_ref[...] = h.astype(BF16)

    o_ref[0] = _dot(h_ref[...], w_ref[...]).astype(o_ref.dtype)


def _mod_linear(xs, mod, gain, w, nctx, i_shift, i_scale, out_dtype):
    B, S, D = xs.shape
    N = w.shape[1]
    tm = _pick_tile(S, 768, 16)
    tn = N if N < 128 else _pick_tile(N, 1024, 128)
    osz = jnp.dtype(out_dtype).itemsize
    est = 2 * (tm * D * 4 + 2 * N_MOD * D * 4 + D * tn * 2 + tm * tn * osz) + tm * D * 2 + 3 * tm * D * 4
    return pl.pallas_call(
        functools.partial(_mod_linear_kernel, tm=tm, nctx=nctx, i_shift=i_shift, i_scale=i_scale),
        grid=(B, S // tm, N // tn),
        in_specs=[
            pl.BlockSpec((1, tm, D), lambda b, i, j: (b, i, 0)),
            pl.BlockSpec((1, 2, N_MOD, D), lambda b, i, j: (b, 0, 0, 0)),
            pl.BlockSpec((1, D), lambda b, i, j: (0, 0)),
            pl.BlockSpec((D, tn), lambda b, i, j: (0, j)),
        ],
        out_specs=pl.BlockSpec((1, tm, tn), lambda b, i, j: (b, i, j)),
        out_shape=jax.ShapeDtypeStruct((B, S, N), out_dtype),
        scratch_shapes=[pltpu.VMEM((tm, D), BF16)],
        compiler_params=_params(("arbitrary", "arbitrary", "arbitrary"), est),
        name="mod_linear",
    )(xs, mod, gain.reshape(1, D), w)


def _out_linear_kernel(y_ref, w_ref, x_ref, mod_ref, o_ref, *, tm, nctx, i_gate):
    t = pl.program_id(1) * tm + lax.broadcasted_iota(jnp.int32, (tm, 1), 0)
    gate = _segment_rows(mod_ref, i_gate, t < nctx)
    o_ref[0] = x_ref[0] + gate * _dot(y_ref[0], w_ref[...])


def _out_linear(y, w, xs, mod, nctx, i_gate):
    B, S, K = y.shape
    D = w.shape[1]
    tm = _pick_tile(S, 768, 16)
    tn = _pick_tile(D, 512, 128)
    est = 2 * (tm * K * 2 + K * tn * 2 + 2 * tm * tn * 4 + 2 * N_MOD * tn * 4) + 2 * tm * tn * 4
    return pl.pallas_call(
        functools.partial(_out_linear_kernel, tm=tm, nctx=nctx, i_gate=i_gate),
        grid=(B, S // tm, D // tn),
        in_specs=[
            pl.BlockSpec((1, tm, K), lambda b, i, j: (b, i, 0)),
            pl.BlockSpec((K, tn), lambda b, i, j: (0, j)),
            pl.BlockSpec((1, tm, tn), lambda b, i, j: (b, i, j)),
            pl.BlockSpec((1, 2, N_MOD, tn), lambda b, i, j: (b, 0, 0, j)),
        ],
        out_specs=pl.BlockSpec((1, tm, tn), lambda b, i, j: (b, i, j)),
        out_shape=jax.ShapeDtypeStruct((B, S, D), F32),
        compiler_params=_params(("arbitrary", "arbitrary", "arbitrary"), est),
        name="out_linear",
    )(y, w, xs, mod)


def _ffn_kernel(xp_ref, x_ref, xn_ref, mod_ref, g_ref, wa_ref, wb_ref, ca_ref, cb_ref, wd_ref, o_ref, h_ref,
                *, tm, nctx, S):
    i, f = pl.program_id(1), pl.program_id(2)
    row0 = i * tm
    gain = g_ref[...]

    @pl.when(f == 0)
    def _():
        h_ref[0:HALO] = _modulate(xp_ref[0], row0 - HALO, nctx, gain, mod_ref, 3, 4).astype(BF16)
        h_ref[HALO:HALO + tm] = _modulate(x_ref[0], row0, nctx, gain, mod_ref, 3, 4).astype(BF16)
        h_ref[HALO + tm:] = _modulate(xn_ref[0], row0 + tm, nctx, gain, mod_ref, 3, 4).astype(BF16)

    t = row0 + lax.broadcasted_iota(jnp.int32, (tm, 1), 0)
    has_prev = jnp.logical_and(t != 0, t != nctx).astype(F32)
    has_next = jnp.logical_and(t != nctx - 1, t != S - 1).astype(F32)
    rows = tm + 2 * HALO

    def conv(w_ref, c_ref):
        u = _dot(h_ref[...], w_ref[...])
        prev = pltpu.roll(u, 1, 0)[HALO:HALO + tm] * has_prev
        nxt = pltpu.roll(u, rows - 1, 0)[HALO:HALO + tm] * has_next
        return c_ref[0:1, :] * prev + c_ref[1:2, :] * u[HALO:HALO + tm] + c_ref[2:3, :] * nxt

    act = (_silu(conv(wa_ref, ca_ref)) * conv(wb_ref, cb_ref)).astype(BF16)
    y = _dot(act, wd_ref[...])

    @pl.when(f == 0)
    def _():
        o_ref[0] = y

    @pl.when(f > 0)
    def _():
        o_ref[0] += y

    @pl.when(f == pl.num_programs(2) - 1)
    def _():
        gate = _segment_rows(mod_ref, 5, t < nctx)
        o_ref[0] = x_ref[0] + gate * o_ref[0]


def _conv_ffn(xs, mod, gain, w_up, conv_w, w_down, nctx):
    B, S, D = xs.shape
    ffp = w_down.shape[0]
    nf = ffp // FFN_TILE
    tm = _pick_tile(S, 576, HALO)
    nh = tm // HALO
    last = S // HALO - 1
    tf = FFN_TILE
    est = (2 * (tm * D * 4 + 2 * HALO * D * 4 + 2 * N_MOD * D * 4 + 2 * D * tf * 2 + 2 * 8 * tf * 4 + tf * D * 2 + tm * D * 4)
           + (tm + 2 * HALO) * D * 2 + 8 * (tm + 2 * HALO) * tf * 4)
    return pl.pallas_call(
        functools.partial(_ffn_kernel, tm=tm, nctx=nctx, S=S),
        grid=(B, S // tm, nf),
        in_specs=[
            pl.BlockSpec((1, HALO, D), lambda b, i, f: (b, jnp.maximum(i * nh - 1, 0), 0)),
            pl.BlockSpec((1, tm, D), lambda b, i, f: (b, i, 0)),
            pl.BlockSpec((1, HALO, D), lambda b, i, f: (b, jnp.minimum((i + 1) * nh, last), 0)),
            pl.BlockSpec((1, 2, N_MOD, D), lambda b, i, f: (b, 0, 0, 0)),
            pl.BlockSpec((1, D), lambda b, i, f: (0, 0)),
            pl.BlockSpec((D, tf), lambda b, i, f: (0, f)),
            pl.BlockSpec((D, tf), lambda b, i, f: (0, nf + f)),
            pl.BlockSpec((FFN_CONV, tf), lambda b, i, f: (0, f)),
            pl.BlockSpec((FFN_CONV, tf), lambda b, i, f: (0, nf + f)),
            pl.BlockSpec((tf, D), lambda b, i, f: (f, 0)),
        ],
        out_specs=pl.BlockSpec((1, tm, D), lambda b, i, f: (b, i, 0)),
        out_shape=jax.ShapeDtypeStruct((B, S, D), F32),
        scratch_shapes=[pltpu.VMEM((tm + 2 * HALO, D), BF16)],
        compiler_params=_params(("arbitrary", "arbitrary", "arbitrary"), est),
        name="conv_ffn",
    )(xs, xs, xs, mod, gain.reshape(1, D), w_up, w_up, conv_w, conv_w, w_down)


def _ffn_weights(w_up, conv_w, w_down):
    ff = w_down.shape[0]
    ffp = -(-ff // FFN_TILE) * FFN_TILE
    pad = ffp - ff

    def halves(a):
        return jnp.concatenate([jnp.pad(a[:, :ff], ((0, 0), (0, pad))), jnp.pad(a[:, ff:], ((0, 0), (0, pad)))], axis=1)

    return halves(w_up).astype(BF16), halves(conv_w), jnp.pad(w_down, ((0, pad), (0, 0))).astype(BF16)


def _level_tables(C, reverse):
    n_lev = int(np.log2(C))
    t = np.arange(C)
    mats, masks = [], []
    for lev in range(n_lev):
        half = 1 << lev
        parent = t // (2 * half)
        mid = parent * 2 * half + half
        second = (t % (2 * half)) >= half
        a = np.zeros((C, C), np.float32)
        for tt in range(C):
            if second[tt]:
                a[tt, mid[tt]:tt + 1] = 1.0
            else:
                a[tt, tt + 1:mid[tt]] = 1.0
        mats.append(a)
        masks.append(((parent[:, None] == parent[None, :]) & second[:, None] & ~second[None, :]).astype(np.float32))
    mats.append(np.tril(np.ones((C, C), np.float32)))
    mats.append(np.ones((C, C), np.float32))
    mats, masks = np.stack(mats), np.stack(masks)
    if reverse:
        mats, masks = mats[:, ::-1, ::-1], masks[:, ::-1, ::-1]
    return mats.reshape(-1, C), masks


def _chunk_index(i, d, nch, ncc):
    if d == 0:
        return i
    return jnp.where(i < ncc, ncc - 1 - i, nch - 1 - i + ncc)


def _hgrn2_kernel(q_ref, v_ref, f0_ref, f1_ref, gt_ref, lb_ref, ng_ref, cm_ref, mk_ref, y_ref, o_acc, st_ref,
                  *, S, nctx, hb):
    C = SCAN_CHUNK
    nch, ncc = S // C, nctx // C
    n_lev = mk_ref.shape[1]
    f_refs = (f0_ref, f1_ref)
    st_ref[...] = jnp.zeros(st_ref.shape, F32)

    def zero(i, carry):
        o_acc[pl.ds(pl.multiple_of(i * C, C), C), :] = jnp.zeros((C, o_acc.shape[1]), F32)
        return carry

    lax.fori_loop(0, nch, zero, 0)

    def chunk(i, carry):
        rows = [pl.ds(pl.multiple_of(_chunk_index(i, d, nch, ncc) * C, C), C) for d in range(2)]
        q, k, v, seg = [], [], [], []
        for d in range(2):
            pf = f_refs[d][0, rows[d], :]
            a = lb_ref[d, 0:1, :]
            b = lb_ref[d, 1:2, :] + (jnp.minimum(pf, 0.0) - jnp.log1p(jnp.exp(-jnp.abs(pf))))
            logf = jnp.maximum(a, b) + jnp.log1p(jnp.exp(-jnp.abs(a - b)))
            k.append(lb_ref[d, 2:3, :] * jax.nn.sigmoid(-pf))
            q.append(_silu(q_ref[0, rows[d], :]))
            v.append(v_ref[0, rows[d], :])
            seg.append(_dot(cm_ref[d], jnp.concatenate(_split_bf16(logf, 2), axis=0)))
        inst = [(d, h, slice(h * HEAD_DIM, (h + 1) * HEAD_DIM)) for d in range(2) for h in range(hb)]
        qh = [q[d][:, sl] for d, _, sl in inst]
        kh = [k[d][:, sl] for d, _, sl in inst]
        vh = [v[d][:, sl] for d, _, sl in inst]
        sg = [seg[d][:, sl] for d, _, sl in inst]
        vb = [x.astype(BF16) for x in vh]
        st = [st_ref[d, h] for d, h, _ in inst]
        cum = [x[n_lev * C:(n_lev + 1) * C] for x in sg]
        tot = [x[(n_lev + 1) * C:] for x in sg]
        o_inter = [_dot_nt((a * jnp.exp(c)).astype(BF16), s.astype(BF16)) for a, c, s in zip(qh, cum, st)]
        pair = []
        for a, b, x in zip(qh, kh, sg):
            e = [jnp.exp(x[lev * C:(lev + 1) * C]) for lev in range(n_lev)]
            pair.append([_dot_nt((a * ee).astype(BF16), (b * ee).astype(BF16)) for ee in e])
        att = [functools.reduce(jnp.add, [mk_ref[d, lev] * p[lev] for lev in range(n_lev)]) for (d, _, _), p in zip(inst, pair)]
        o_intra = [_dot(a.astype(BF16), x) for a, x in zip(att, vb)]
        upd = [_dot_tn(x, (b * jnp.exp(t - c)).astype(BF16)) for x, b, t, c in zip(vb, kh, tot, cum)]
        for n, (d, h, sl) in enumerate(inst):
            o_acc[rows[d], sl] += o_intra[n] + o_inter[n] + jnp.sum(qh[n] * kh[n], axis=-1, keepdims=True) * vh[n]
            st_ref[d, h] = st[n] * jnp.exp(tot[n][0:1, :]) + upd[n]
        return carry

    lax.fori_loop(0, nch, chunk, 0, unroll=2)

    def readout(i, carry):
        rows = pl.ds(pl.multiple_of(i * C, C), C)
        for h in range(hb):
            sl = slice(h * HEAD_DIM, (h + 1) * HEAD_DIM)
            o = o_acc[rows, sl]
            y = o * lax.rsqrt(jnp.mean(o * o, axis=-1, keepdims=True) + RMS_EPS) * ng_ref[...]
            y_ref[0, rows, sl] = (y * _silu(gt_ref[0, rows, sl])).astype(BF16)
        return carry

    lax.fori_loop(0, nch, readout, 0)


def _hgrn2_scan(p, lb, norm_g, nctx):
    B, S, W5 = p.shape
    W = W5 // 5
    hb = 2 if W % (2 * HEAD_DIM) == 0 else 1
    wb = hb * HEAD_DIM
    nhg = W // wb
    C = SCAN_CHUNK
    tabs = [_level_tables(C, rev) for rev in (False, True)]
    cm = jnp.asarray(np.stack([np.concatenate([t[0], t[0]], axis=1) for t in tabs]), BF16)
    mk = jnp.asarray(np.stack([t[1] for t in tabs]), F32)
    lbp = jnp.stack([jnp.log(lb), jnp.log1p(-lb), 1.0 - lb], axis=1)
    est = 2 * (5 * S * wb * 4 + S * wb * 2) + S * wb * 4 + 16 * C * wb * 4 * 8
    col = lambda k: pl.BlockSpec((1, S, wb), lambda b, g, k=k: (b, 0, k * nhg + g))
    return pl.pallas_call(
        functools.partial(_hgrn2_kernel, S=S, nctx=nctx, hb=hb),
        grid=(B, nhg),
        in_specs=[
            col(0), col(1), col(2), col(3), col(4),
            pl.BlockSpec((2, 3, wb), lambda b, g: (0, 0, g)),
            pl.BlockSpec((1, HEAD_DIM), lambda b, g: (0, 0)),
            pl.BlockSpec(cm.shape, lambda b, g: (0, 0, 0)),
            pl.BlockSpec(mk.shape, lambda b, g: (0, 0, 0, 0)),
        ],
        out_specs=pl.BlockSpec((1, S, wb), lambda b, g: (b, 0, g)),
        out_shape=jax.ShapeDtypeStruct((B, S, W), BF16),
        scratch_shapes=[pltpu.VMEM((S, wb), F32), pltpu.VMEM((2, hb, HEAD_DIM, HEAD_DIM), F32)],
        compiler_params=_params(("arbitrary", "arbitrary"), est),
        name="hgrn2_scan",
    )(p, p, p, p, p, lbp, norm_g.reshape(1, HEAD_DIM), cm, mk)


GDN_ROWS = 256


def _softplus(x):
    return jnp.maximum(x, 0.0) + jnp.log1p(jnp.exp(-jnp.abs(x)))


def _conv5_silu(x_ref, w_ref, r0, S, nctx):
    n = GDN_ROWS + 16
    lo = pl.multiple_of(jnp.maximum(r0 - 8, 0), 8)
    hi = pl.multiple_of(jnp.minimum(r0 + GDN_ROWS, S - 8), 8)
    ext = jnp.concatenate([x_ref[0, pl.ds(lo, 8), :], x_ref[0, pl.ds(r0, GDN_ROWS), :], x_ref[0, pl.ds(hi, 8), :]], axis=0)
    t = r0 + lax.broadcasted_iota(jnp.int32, (GDN_ROWS, 1), 0)
    pos = jnp.where(t < nctx, t, t - nctx)
    length = jnp.where(t < nctx, nctx, S - nctx)
    half = GDN_CONV // 2
    acc = w_ref[half:half + 1, :] * ext[8:8 + GDN_ROWS]
    for off in range(-half, half + 1):
        if off == 0:
            continue
        valid = jnp.logical_and(pos + off >= 0, pos + off < length).astype(F32)
        tap = pltpu.roll(ext, (-off) % n, 0)[8:8 + GDN_ROWS]
        acc = acc + (w_ref[off + half:off + half + 1, :] * valid) * tap
    return _silu(acc)


def _l2norm(x):
    return x * lax.rsqrt(jnp.sum(x * x, axis=-1, keepdims=True) + RMS_EPS)


GDN_GROUP = 4


def _gdn_kernel(pq_ref, pk_ref, pv_ref, pz_ref, bag_ref, bat_ref, cq_ref, ck_ref, cv_ref, gl_ref, gc_ref, ng_ref,
                cmc_ref, cmr_ref, mk_ref, y_ref, qn_ref, kn_ref, vv_ref, o_acc, st_ref, *, S, nctx):
    C, G = SCAN_CHUNK, GDN_GROUP
    nch, ncc = S // C, nctx // C
    ngrp, ncg = nch // G, ncc // G
    n_lev = mk_ref.shape[1] - 2

    def prologue(i, carry):
        r0 = pl.multiple_of(i * GDN_ROWS, GDN_ROWS)
        rows = pl.ds(r0, GDN_ROWS)
        qn_ref[rows, :] = _l2norm(_conv5_silu(pq_ref, cq_ref, r0, S, nctx)) * (HEAD_DIM ** -0.5)
        kn_ref[rows, :] = _l2norm(_conv5_silu(pk_ref, ck_ref, r0, S, nctx))
        vv_ref[rows, :] = _conv5_silu(pv_ref, cv_ref, r0, S, nctx)
        o_acc[rows, :] = jnp.zeros((GDN_ROWS, 2 * HEAD_DIM), F32)
        return carry

    lax.fori_loop(0, S // GDN_ROWS, prologue, 0)
    st_ref[...] = jnp.zeros(st_ref.shape, F32)

    def group_terms(chunks):
        rows = [pl.ds(pl.multiple_of(c * C, C), C) for c, _ in chunks]
        ba = [bag_ref[0, 0, r, :] for r in rows]
        beta8 = [jax.nn.sigmoid(x) for x in ba]
        g8 = [gl_ref[0, 0:1, :] * _softplus(x + gl_ref[0, 1:2, :]) for x in ba]
        g8r = [gc_ref[0, :, 0:1] * _softplus(bat_ref[0, 0, c] + gc_ref[0, :, 1:2]) for c, _ in chunks]
        cum8 = [_dot(cmc_ref[d], jnp.concatenate(_split_bf16(x, 3), axis=0)) for x, (_, d) in zip(g8, chunks)]
        cum8r = [_dot(jnp.concatenate(_split_bf16(x, 3), axis=1), cmr_ref[d]) for x, (_, d) in zip(g8r, chunks)]
        kc = [kn_ref[r, :] for r in rows]
        qc = [qn_ref[r, :] for r in rows]
        kb = [x.astype(BF16) for x in kc]
        kk = [_dot_nt(x, x) for x in kb]
        qk = [_dot_nt(x.astype(BF16), y) for x, y in zip(qc, kb)]
        inst = [(i, h) for i in range(len(chunks)) for h in range(2)]
        beta, gcum, gtot, dec_i, lmat, tinv = [], [], [], [], [], []
        for i, h in inst:
            d = chunks[i][1]
            lb, la = 4 * d + h, 4 * d + 2 + h
            strict, incl = mk_ref[d, n_lev], mk_ref[d, n_lev + 1]
            beta.append(beta8[i][:, lb:lb + 1])
            gcum.append(cum8[i][:C, la:la + 1])
            gtot.append(cum8[i][C:, la:la + 1])
            diff = gcum[-1] - cum8r[i][la:la + 1, :]
            dec_i.append(jnp.exp(jnp.where(incl > 0.5, diff, NEG_BIG)))
            lmat.append(beta[-1] * kk[i] * jnp.exp(jnp.where(strict > 0.5, diff, NEG_BIG)))
            tinv.append((incl - strict) - lmat[-1] * mk_ref[d, 0])
        for lev in range(1, n_lev):
            tb = [t.astype(BF16) for t in tinv]
            x = [_dot(t, (lm * mk_ref[chunks[i][1], lev]).astype(BF16)).astype(BF16) for t, lm, (i, _) in zip(tb, lmat, inst)]
            tinv = [t - _dot(xi, ti) for t, xi, ti in zip(tinv, x, tb)]
        rhs = [jnp.concatenate([vv_ref[rows[i], h * HEAD_DIM:(h + 1) * HEAD_DIM] * b, kc[i] * (b * jnp.exp(g))], axis=1)
               for (i, h), b, g in zip(inst, beta, gcum)]
        sol = [_dot(t.astype(BF16), r.astype(BF16)).astype(BF16) for t, r in zip(tinv, rhs)]
        ku_kw = [_dot_tn((kc[i] * jnp.exp(gt - g)).astype(BF16), s) for (i, _), gt, g, s in zip(inst, gtot, gcum, sol)]
        au_aw = [_dot((qk[i] * dc).astype(BF16), s) for (i, _), dc, s in zip(inst, dec_i, sol)]
        out = {}
        for n, (i, h) in enumerate(inst):
            qt = qc[i] * jnp.exp(gcum[n]) - au_aw[n][:, HEAD_DIM:]
            lhs = jnp.concatenate([ku_kw[n][:, HEAD_DIM:], qt], axis=0).astype(BF16)
            out[i, h] = (lhs, ku_kw[n][:, :HEAD_DIM], au_aw[n][:, :HEAD_DIM], jnp.exp(gtot[n][0:1, :]))
        return out

    def group(gi, carry):
        base = [_chunk_index(gi, d, ngrp, ncg) * G for d in range(2)]
        terms = group_terms([(base[d] + j, d) for d in range(2) for j in range(G)])
        chains = [(d, h) for d in range(2) for h in range(2)]
        st = [st_ref[d, h] for d, h in chains]
        for step in range(G):
            js = [step if d == 0 else G - 1 - step for d, _ in chains]
            r = [_dot(terms[d * G + j, h][0], s.astype(BF16)) for j, (d, h), s in zip(js, chains, st)]
            for n, (j, (d, h)) in enumerate(zip(js, chains)):
                _, ku, au, decay = terms[d * G + j, h]
                rows = pl.ds(pl.multiple_of((base[d] + j) * C, C), C)
                o_acc[rows, h * HEAD_DIM:(h + 1) * HEAD_DIM] += r[n][HEAD_DIM:] + au
                st[n] = decay * st[n] - r[n][:HEAD_DIM] + ku
        for s, (d, h) in zip(st, chains):
            st_ref[d, h] = s
        return carry

    lax.fori_loop(0, ngrp, group, 0)

    def readout(i, carry):
        rows = pl.ds(pl.multiple_of(i * C, C), C)
        for h in range(2):
            sl = slice(h * HEAD_DIM, (h + 1) * HEAD_DIM)
            o = o_acc[rows, sl]
            y = o * lax.rsqrt(jnp.mean(o * o, axis=-1, keepdims=True) + RMS_EPS) * ng_ref[...]
            y_ref[0, rows, sl] = (y * _silu(pz_ref[0, rows, sl])).astype(BF16)
        return carry

    lax.fori_loop(0, nch, readout, 0)


def _gdn_scan(p, ba, conv_w, a_log, dt_bias, norm_g, nctx):
    B, S, W6 = p.shape
    W = W6 // 6
    nq = W // HEAD_DIM
    C, G = SCAN_CHUNK, GDN_GROUP
    assert ba.shape[2] == 8 * nq and S % GDN_ROWS == 0 and nctx % GDN_ROWS == 0 and S % (C * G) == 0 and nctx % (C * G) == 0
    n_lev = int(np.log2(C))
    tri, masks = [], []
    for rev in (False, True):
        mats, mk = _level_tables(C, rev)
        tri.append(mats[n_lev * C:])
        strict = mk.sum(0)
        masks.append(np.concatenate([mk, strict[None], (strict + np.eye(C, dtype=np.float32))[None]], axis=0))
    cmc = jnp.asarray(np.stack([np.concatenate([t, t, t], axis=1) for t in tri]), BF16)
    cmr = jnp.asarray(np.stack([np.concatenate([t[:C].T, t[:C].T, t[:C].T], axis=0) for t in tri]), BF16)
    mk = jnp.asarray(np.stack(masks), F32)

    def per_group(a):
        a = a.reshape(a.shape[:-1] + (nq, 2))
        return jnp.moveaxis(a, -2, 0).reshape((nq,) + a.shape[:-4] + (8,))

    bag = jnp.moveaxis(per_group(ba.reshape(B, S, 2, 2, 2 * nq)), 0, 1)
    bat = jnp.swapaxes(bag.reshape(B, nq, S // C, C, 8), 3, 4)
    zeros = jnp.zeros((2, 2 * nq), F32)
    neg_rate = per_group(jnp.stack([zeros, -jnp.exp(a_log.astype(F32))], axis=1))
    dtb = per_group(jnp.stack([zeros, dt_bias.astype(F32)], axis=1))
    gl = jnp.stack([neg_rate, dtb], axis=1)
    q128, q256 = W // HEAD_DIM, W // (2 * HEAD_DIM)
    est = (2 * (2 * S * 128 * 4 + 2 * S * 256 * 4 + S * 128 * 4 + (S // C) * 8 * 128 * 4 + S * 256 * 2)
           + 2 * S * 128 * 4 + 2 * S * 256 * 4 + 16 * (3 * 128 * 128 * 4))
    return pl.pallas_call(
        functools.partial(_gdn_kernel, S=S, nctx=nctx),
        grid=(B, nq),
        in_specs=[
            pl.BlockSpec((1, S, HEAD_DIM), lambda b, g: (b, 0, g)),
            pl.BlockSpec((1, S, HEAD_DIM), lambda b, g: (b, 0, q128 + g)),
            pl.BlockSpec((1, S, 2 * HEAD_DIM), lambda b, g: (b, 0, 2 * q256 + g)),
            pl.BlockSpec((1, S, 2 * HEAD_DIM), lambda b, g: (b, 0, 4 * q256 + g)),
            pl.BlockSpec((1, 1, S, 8), lambda b, g: (b, g, 0, 0)),
            pl.BlockSpec((1, 1, S // C, 8, C), lambda b, g: (b, g, 0, 0, 0)),
            pl.BlockSpec((GDN_CONV, HEAD_DIM), lambda b, g: (0, g)),
            pl.BlockSpec((GDN_CONV, HEAD_DIM), lambda b, g: (0, q128 + g)),
            pl.BlockSpec((GDN_CONV, 2 * HEAD_DIM), lambda b, g: (0, 2 * q256 + g)),
            pl.BlockSpec((1, 2, 8), lambda b, g: (g, 0, 0)),
            pl.BlockSpec((1, 8, 2), lambda b, g: (g, 0, 0)),
            pl.BlockSpec((1, HEAD_DIM), lambda b, g: (0, 0)),
            pl.BlockSpec(cmc.shape, lambda b, g: (0, 0, 0)),
            pl.BlockSpec(cmr.shape, lambda b, g: (0, 0, 0)),
            pl.BlockSpec(mk.shape, lambda b, g: (0, 0, 0, 0)),
        ],
        out_specs=pl.BlockSpec((1, S, 2 * HEAD_DIM), lambda b, g: (b, 0, g)),
        out_shape=jax.ShapeDtypeStruct((B, S, 2 * W), BF16),
        scratch_shapes=[pltpu.VMEM((S, HEAD_DIM), F32), pltpu.VMEM((S, HEAD_DIM), F32), pltpu.VMEM((S, 2 * HEAD_DIM), F32),
                        pltpu.VMEM((S, 2 * HEAD_DIM), F32), pltpu.VMEM((2, 2, HEAD_DIM, HEAD_DIM), F32)],
        compiler_params=_params(("arbitrary", "arbitrary"), est),
        name="gdn_scan",
    )(p, p, p, p, bag, bat, conv_w, conv_w, conv_w, gl, jnp.swapaxes(gl, 1, 2), norm_g.reshape(1, HEAD_DIM), cmc, cmr, mk)


NA_ROWS = 256
NA_PAIRS = NA_WIN_R // 2
NA_UNROLL = 2


def _na_bias_table(rpb):
    q = np.arange(GRID_W)[:, None]
    kc = np.arange(GRID_W)[None, :]
    c0 = np.clip(q - NA_WIN_C // 2, 0, GRID_W - NA_WIN_C)
    valid = (kc >= c0) & (kc < c0 + NA_WIN_C)
    idx = np.clip(kc - q + NA_WIN_C - 1, 0, 2 * NA_WIN_C - 2)
    t = jnp.where(valid, rpb.astype(F32)[:, :, idx], NEG_BIG)
    return jnp.concatenate([t[:, :-1], t[:, 1:]], axis=-1)


def _softmax_pv(scores, values):
    m = functools.reduce(jnp.maximum, [jnp.max(s, axis=-1, keepdims=True) for s in scores])
    ps = [jnp.exp(s - m) for s in scores]
    den = functools.reduce(jnp.add, [jnp.sum(p, axis=-1, keepdims=True) for p in ps])
    num = functools.reduce(jnp.add, [_dot(p.astype(BF16), v) for p, v in zip(ps, values)])
    return num / den


def _na_kernel(q_ref, k_ref, v_ref, qg_ref, kg_ref, gm_ref, hm_ref, t2_ref, o_ref, qn_ref, kn_ref, vb_ref,
               *, S, nctx, grid_rows):
    scale = NA_HEAD_DIM ** -0.5

    def head_rms(x):
        hi, lo = _split_bf16(x * x, 2)
        ms = _dot(hi, gm_ref[...]) + _dot(lo, gm_ref[...])
        return x * lax.rsqrt(ms + RMS_EPS)

    def prologue(i, carry):
        rows = pl.ds(pl.multiple_of(i * NA_ROWS, NA_ROWS), NA_ROWS)
        qn_ref[rows, :] = (head_rms(q_ref[0, rows, :]) * (qg_ref[...] * scale)).astype(BF16)
        kn_ref[rows, :] = (head_rms(k_ref[0, rows, :]) * kg_ref[...]).astype(BF16)
        vb_ref[rows, :] = v_ref[0, rows, :].astype(BF16)
        return carry

    lax.fori_loop(0, S // NA_ROWS, prologue, 0)

    def attend(blocks):
        q4 = [jnp.concatenate([qn_ref[rows, :] * hm_ref[j].astype(BF16) for j in range(NA_GROUP)], axis=0)
              for rows, _ in blocks]
        scores = [[_dot_nt(q, k) if bias is None else _dot_nt(q, k) + bias for k, _, bias in parts]
                  for q, (_, parts) in zip(q4, blocks)]
        for (rows, parts), sc in zip(blocks, scores):
            o4 = _softmax_pv(sc, [v for _, v, _ in parts])
            out = functools.reduce(jnp.add, [o4[j * GRID_W:(j + 1) * GRID_W] * hm_ref[j] for j in range(NA_GROUP)])
            o_ref[0, rows, :] = out.astype(BF16)

    k_ctx, v_ctx = kn_ref[0:nctx, :], vb_ref[0:nctx, :]
    attend([(pl.ds(i * GRID_W, GRID_W), [(k_ctx, v_ctx, None)]) for i in range(nctx // GRID_W)])

    def grid_rows_step(i, carry):
        blocks = []
        for r in (NA_UNROLL * i + u for u in range(NA_UNROLL)):
            r0 = jnp.clip(r - NA_WIN_R // 2, 0, grid_rows - NA_WIN_R)
            dr0 = r0 - r + NA_WIN_R - 1
            rows = pl.ds(pl.multiple_of(nctx + r * GRID_W, GRID_W), GRID_W)
            keys = pl.ds(pl.multiple_of(nctx + r0 * GRID_W, GRID_W), NA_WIN_R * GRID_W)
            bias = jnp.concatenate([t2_ref[0, dr0 + 2 * m] for m in range(NA_PAIRS)], axis=1)
            blocks.append((rows, [(kn_ref[keys, :], vb_ref[keys, :], bias), (kn_ref[0:nctx, :], vb_ref[0:nctx, :], None)]))
        attend(blocks)
        return carry

    lax.fori_loop(0, grid_rows // NA_UNROLL, grid_rows_step, 0)


def _na_attention(p, q_gain, k_gain, rpb, nctx):
    B, S, D3 = p.shape
    D = D3 // 3
    grid_rows = (S - nctx) // GRID_W
    assert (S - nctx) % GRID_W == 0 and grid_rows >= NA_WIN_R and S % NA_ROWS == 0 and nctx % GRID_W == 0
    ng = D // HEAD_DIM
    assert grid_rows % NA_UNROLL == 0
    t2 = _na_bias_table(rpb).reshape(ng, NA_GROUP, 2 * NA_WIN_R - 2, GRID_W, 2 * GRID_W)
    t2 = jnp.swapaxes(t2, 1, 2).reshape(ng, 2 * NA_WIN_R - 2, NA_GROUP * GRID_W, 2 * GRID_W)
    lane_head = np.arange(HEAD_DIM) // NA_HEAD_DIM
    gm = jnp.asarray((lane_head[:, None] == lane_head[None, :]) / NA_HEAD_DIM, BF16)
    hm = jnp.asarray((np.arange(NA_GROUP)[:, None, None] == lane_head[None, None, :]), F32)
    tile = lambda g: jnp.tile(g.astype(F32), NA_GROUP).reshape(1, HEAD_DIM)
    est = (2 * (3 * S * 128 * 4 + t2.shape[1] * t2.shape[2] * 128 * 4 + S * 128 * 2) + 3 * S * 128 * 2
           + 6 * NA_UNROLL * NA_GROUP * GRID_W * (NA_WIN_R * GRID_W + nctx) * 4)
    col = lambda k: pl.BlockSpec((1, S, HEAD_DIM), lambda b, g, k=k: (b, 0, k * ng + g))
    return pl.pallas_call(
        functools.partial(_na_kernel, S=S, nctx=nctx, grid_rows=grid_rows),
        grid=(B, ng),
        in_specs=[
            col(0), col(1), col(2),
            pl.BlockSpec((1, HEAD_DIM), lambda b, g: (0, 0)),
            pl.BlockSpec((1, HEAD_DIM), lambda b, g: (0, 0)),
            pl.BlockSpec(gm.shape, lambda b, g: (0, 0)),
            pl.BlockSpec(hm.shape, lambda b, g: (0, 0, 0)),
            pl.BlockSpec((1,) + t2.shape[1:], lambda b, g: (g, 0, 0, 0)),
        ],
        out_specs=pl.BlockSpec((1, S, HEAD_DIM), lambda b, g: (b, 0, g)),
        out_shape=jax.ShapeDtypeStruct((B, S, D), BF16),
        scratch_shapes=[pltpu.VMEM((S, HEAD_DIM), BF16)] * 3,
        compiler_params=_params(("arbitrary", "arbitrary"), est),
        name="na_attention",
    )(p, p, p, tile(q_gain), tile(k_gain), gm, hm, t2)


def _hgrn2_lower_bound(lb_logits, j):
    cs = jnp.cumsum(jax.nn.softmax(lb_logits.astype(F32), axis=0), axis=0)
    return cs[j] - cs[0]


def kernel(x, c, ctx, c_ctx, ada_w, ada_b, norm_mix_g, norm_ffn_g, hg_w_in, hg_lb_logits, hg_norm_g, hg_w_out,
           gdn_w_in, gdn_conv_w, gdn_a_log, gdn_dt_bias, gdn_norm_g, gdn_w_out, na_w_qkv, na_q_norm_g, na_k_norm_g,
           na_rpb, na_w_out, ffn_w_up, ffn_conv_w, ffn_w_down):
    depth = ada_w.shape[0]
    nctx = ctx.shape[1]
    D = x.shape[2]
    xs = jnp.concatenate([ctx, x], axis=1)
    mods = _ada_params(c, c_ctx, ada_w, ada_b)
    for i in range(depth):
        m, j = i % 3, i // 3
        mod = mods[i]
        if m == 0:
            p = _mod_linear(xs, mod, norm_mix_g[i], hg_w_in[j].astype(BF16), nctx, 0, 1, F32)
            y = _hgrn2_scan(p, _hgrn2_lower_bound(hg_lb_logits, j), hg_norm_g[j], nctx)
            w_out = hg_w_out[j]
        elif m == 1:
            w_in = gdn_w_in[j].astype(BF16)
            p = _mod_linear(xs, mod, norm_mix_g[i], w_in[:, :6 * D], nctx, 0, 1, F32)
            ba = _mod_linear(xs, mod, norm_mix_g[i], w_in[:, 6 * D:], nctx, 0, 1, F32)
            y = _gdn_scan(p, ba, gdn_conv_w[j], gdn_a_log[j], gdn_dt_bias[j], gdn_norm_g[j], nctx)
            w_out = gdn_w_out[j]
        else:
            p = _mod_linear(xs, mod, norm_mix_g[i], na_w_qkv[j].astype(BF16), nctx, 0, 1, F32)
            y = _na_attention(p, na_q_norm_g[j], na_k_norm_g[j], na_rpb[j], nctx)
            w_out = na_w_out[j]
        xs = _out_linear(y, w_out.astype(BF16), xs, mod, nctx, 2)
        w_up, conv_w, w_down = _ffn_weights(ffn_w_up[i], ffn_conv_w[i], ffn_w_down[i])
        xs = _conv_ffn(xs, mod, norm_ffn_g[i], w_up, conv_w, w_down, nctx)
    return xs[:, nctx:]
```

```python
import functools

import numpy as np
import jax
import jax.numpy as jnp
from jax import lax
from jax.experimental import pallas as pl
from jax.experimental.pallas import tpu as pltpu

F32 = jnp.float32
BF16 = jnp.bfloat16

RMS_EPS = 1e-6
N_MOD = 6
HEAD_DIM = 128
SCAN_CHUNK = 64
GRID_W = 64
NA_HEAD_DIM = 32
NA_WIN_R = 8
NA_WIN_C = 16
NA_GROUP = HEAD_DIM // NA_HEAD_DIM
GDN_CONV = 5
FFN_CONV = 3
FFN_TILE = 512
HALO = 16
NEG_BIG = -1e30

V7X_VMEM_BYTES = 64 * 1024 * 1024
VMEM_LIMIT_CAP = 56 * 1024 * 1024

NT_DIMS = (((1,), (1,)), ((), ()))
TN_DIMS = (((0,), (0,)), ((), ()))


def _params(sem, est_bytes, **kw):
    limit = int(min(VMEM_LIMIT_CAP, max(32 * 1024 * 1024, est_bytes * 5 // 4)))
    return pltpu.CompilerParams(dimension_semantics=sem, vmem_limit_bytes=limit, **kw)


def _pick_tile(n, cap, mult):
    best = None
    for t in range(mult, min(n, cap) + 1, mult):
        if n % t == 0:
            best = t
    assert best is not None, (n, cap, mult)
    return best


def _silu(x):
    return x * jax.nn.sigmoid(x)


def _dot(a, b):
    return jnp.dot(a, b, preferred_element_type=F32)


def _dot_nt(a, b):
    return lax.dot_general(a, b, NT_DIMS, preferred_element_type=F32)


def _dot_tn(a, b):
    return lax.dot_general(a, b, TN_DIMS, preferred_element_type=F32)


def _split_bf16(x, n):
    parts = []
    for _ in range(n):
        p = x.astype(BF16)
        parts.append(p)
        x = x - p.astype(F32)
    return parts


def _segment_rows(mod_ref, idx, is_ctx):
    return jnp.where(is_ctx, mod_ref[0, 0, idx:idx + 1, :], mod_ref[0, 1, idx:idx + 1, :])


def _modulate(x, row0, nctx, gain, mod_ref, i_shift, i_scale):
    ms = jnp.mean(x * x, axis=-1, keepdims=True)
    xn = x * lax.rsqrt(ms + RMS_EPS)
    t = row0 + lax.broadcasted_iota(jnp.int32, (x.shape[0], 1), 0)
    is_ctx = t < nctx
    scale = _segment_rows(mod_ref, i_scale, is_ctx)
    shift = _segment_rows(mod_ref, i_shift, is_ctx)
    return xn * (gain * (1.0 + scale)) + shift


def _ada_kernel(c_ref, w_ref, b_ref, o_ref):
    s = _silu(c_ref[...]).astype(BF16)
    o_ref[0] = _dot(s, w_ref[0]) + b_ref[0]


def _ada_params(c, c_ctx, ada_w, ada_b):
    L, D, N = ada_w.shape
    B = c.shape[0]
    rows = -(-(B + 1) // 16) * 16
    cc = jnp.zeros((rows, D), F32).at[:B].set(c).at[B].set(c_ctx)
    tn = _pick_tile(N, 1024, 128)
    est = 2 * (rows * D * 4 + D * tn * 2 + tn * 4 + rows * tn * 4)
    out = pl.pallas_call(
        _ada_kernel,
        grid=(L, N // tn),
        in_specs=[
            pl.BlockSpec((rows, D), lambda l, j: (0, 0)),
            pl.BlockSpec((1, D, tn), lambda l, j: (l, 0, j)),
            pl.BlockSpec((1, 1, tn), lambda l, j: (l, 0, j)),
        ],
        out_specs=pl.BlockSpec((1, rows, tn), lambda l, j: (l, 0, j)),
        out_shape=jax.ShapeDtypeStruct((L, rows, N), F32),
        compiler_params=_params(("arbitrary", "arbitrary"), est),
        name="ada_ln",
    )(cc, ada_w.astype(BF16), ada_b.reshape(L, 1, N))
    lat = out[:, :B].reshape(L, B, 1, N_MOD, D)
    ctx = jnp.broadcast_to(out[:, B].reshape(L, 1, 1, N_MOD, D), (L, B, 1, N_MOD, D))
    return jnp.concatenate([ctx, lat], axis=2)


def _mod_linear_kernel(x_ref, mod_ref, g_ref, w_ref, o_ref, h_ref, *, tm, nctx, i_shift, i_scale):
    @pl.when(pl.program_id(2) == 0)
    def _():
        h = _modulate(x_ref[0], pl.program_id(1) * tm, nctx, g_ref[...], mod_ref, i_shift, i_scale)
        h_ref[...] = h.astype(BF16)

    o_ref[0] = _dot(h_ref[...], w_ref[...]).astype(o_ref.dtype)


def _mod_linear(xs, mod, gain, w, nctx, i_shift, i_scale, out_dtype):
    B, S, D = xs.shape
    N = w.shape[1]
    tm = _pick_tile(S, 768, 16)
    tn = N if N < 128 else _pick_tile(N, 1024, 128)
    osz = jnp.dtype(out_dtype).itemsize
    est = 2 * (tm * D * 4 + 2 * N_MOD * D * 4 + D * tn * 2 + tm * tn * osz) + tm * D * 2 + 3 * tm * D * 4
    return pl.pallas_call(
        functools.partial(_mod_linear_kernel, tm=tm, nctx=nctx, i_shift=i_shift, i_scale=i_scale),
        grid=(B, S // tm, N // tn),
        in_specs=[
            pl.BlockSpec((1, tm, D), lambda b, i, j: (b, i, 0)),
            pl.BlockSpec((1, 2, N_MOD, D), lambda b, i, j: (b, 0, 0, 0)),
            pl.BlockSpec((1, D), lambda b, i, j: (0, 0)),
            pl.BlockSpec((D, tn), lambda b, i, j: (0, j)),
        ],
        out_specs=pl.BlockSpec((1, tm, tn), lambda b, i, j: (b, i, j)),
        out_shape=jax.ShapeDtypeStruct((B, S, N), out_dtype),
        scratch_shapes=[pltpu.VMEM((tm, D), BF16)],
        compiler_params=_params(("arbitrary", "arbitrary", "arbitrary"), est),
        name="mod_linear",
    )(xs, mod, gain.reshape(1, D), w)


def _out_linear_kernel(y_ref, w_ref, x_ref, mod_ref, o_ref, *, tm, nctx, i_gate):
    t = pl.program_id(1) * tm + lax.broadcasted_iota(jnp.int32, (tm, 1), 0)
    gate = _segment_rows(mod_ref, i_gate, t < nctx)
    o_ref[0] = x_ref[0] + gate * _dot(y_ref[0], w_ref[...])


def _out_linear(y, w, xs, mod, nctx, i_gate):
    B, S, K = y.shape
    D = w.shape[1]
    tm = _pick_tile(S, 768, 16)
    tn = _pick_tile(D, 1024 if K <= 2048 else 512, 128)
    est = 2 * (tm * K * 2 + K * tn * 2 + 2 * tm * tn * 4 + 2 * N_MOD * tn * 4) + 2 * tm * tn * 4
    return pl.pallas_call(
        functools.partial(_out_linear_kernel, tm=tm, nctx=nctx, i_gate=i_gate),
        grid=(B, S // tm, D // tn),
        in_specs=[
            pl.BlockSpec((1, tm, K), lambda b, i, j: (b, i, 0)),
            pl.BlockSpec((K, tn), lambda b, i, j: (0, j)),
            pl.BlockSpec((1, tm, tn), lambda b, i, j: (b, i, j)),
            pl.BlockSpec((1, 2, N_MOD, tn), lambda b, i, j: (b, 0, 0, j)),
        ],
        out_specs=pl.BlockSpec((1, tm, tn), lambda b, i, j: (b, i, j)),
        out_shape=jax.ShapeDtypeStruct((B, S, D), F32),
        compiler_params=_params(("arbitrary", "arbitrary", "arbitrary"), est),
        name="out_linear",
    )(y, w, xs, mod)


def _ffn_up_kernel(xp_ref, x_ref, xn_ref, mod_ref, g_ref, wa_ref, wb_ref, ca_ref, cb_ref, o_ref, h_ref, *, tm, nctx, S):
    i, f = pl.program_id(1), pl.program_id(2)
    row0 = i * tm
    gain = g_ref[...]

    @pl.when(f == 0)
    def _():
        h_ref[0:HALO] = _modulate(xp_ref[0], row0 - HALO, nctx, gain, mod_ref, 3, 4).astype(BF16)
        h_ref[HALO:HALO + tm] = _modulate(x_ref[0], row0, nctx, gain, mod_ref, 3, 4).astype(BF16)
        h_ref[HALO + tm:] = _modulate(xn_ref[0], row0 + tm, nctx, gain, mod_ref, 3, 4).astype(BF16)

    t = row0 + lax.broadcasted_iota(jnp.int32, (tm, 1), 0)
    has_prev = jnp.logical_and(t != 0, t != nctx).astype(F32)
    has_next = jnp.logical_and(t != nctx - 1, t != S - 1).astype(F32)
    rows = tm + 2 * HALO

    def conv(w_ref, c_ref):
        u = _dot(h_ref[...], w_ref[...])
        prev = pltpu.roll(u, 1, 0)[HALO:HALO + tm] * has_prev
        nxt = pltpu.roll(u, rows - 1, 0)[HALO:HALO + tm] * has_next
        return c_ref[0:1, :] * prev + c_ref[1:2, :] * u[HALO:HALO + tm] + c_ref[2:3, :] * nxt

    o_ref[0] = (_silu(conv(wa_ref, ca_ref)) * conv(wb_ref, cb_ref)).astype(BF16)


def _conv_ffn(xs, mod, gain, w_up, conv_w, w_down, nctx):
    B, S, D = xs.shape
    ffp = w_down.shape[0]
    tf = FFN_TILE
    nf = ffp // tf
    tm = _pick_tile(S, 768, HALO)
    nh = tm // HALO
    last = S // HALO - 1
    est = (2 * (tm * D * 4 + 2 * HALO * D * 4 + 2 * N_MOD * D * 4 + 2 * D * tf * 2 + 2 * 8 * tf * 4 + tm * tf * 2)
           + (tm + 2 * HALO) * D * 2 + 10 * (tm + 2 * HALO) * tf * 4)
    hidden = pl.pallas_call(
        functools.partial(_ffn_up_kernel, tm=tm, nctx=nctx, S=S),
        grid=(B, S // tm, nf),
        in_specs=[
            pl.BlockSpec((1, HALO, D), lambda b, i, f: (b, jnp.maximum(i * nh - 1, 0), 0)),
            pl.BlockSpec((1, tm, D), lambda b, i, f: (b, i, 0)),
            pl.BlockSpec((1, HALO, D), lambda b, i, f: (b, jnp.minimum((i + 1) * nh, last), 0)),
            pl.BlockSpec((1, 2, N_MOD, D), lambda b, i, f: (b, 0, 0, 0)),
            pl.BlockSpec((1, D), lambda b, i, f: (0, 0)),
            pl.BlockSpec((D, tf), lambda b, i, f: (0, f)),
            pl.BlockSpec((D, tf), lambda b, i, f: (0, nf + f)),
            pl.BlockSpec((FFN_CONV, tf), lambda b, i, f: (0, f)),
            pl.BlockSpec((FFN_CONV, tf), lambda b, i, f: (0, nf + f)),
        ],
        out_specs=pl.BlockSpec((1, tm, tf), lambda b, i, f: (b, i, f)),
        out_shape=jax.ShapeDtypeStruct((B, S, ffp), BF16),
        scratch_shapes=[pltpu.VMEM((tm + 2 * HALO, D), BF16)],
        compiler_params=_params(("arbitrary", "arbitrary", "arbitrary"), est),
        name="ffn_up",
    )(xs, xs, xs, mod, gain.reshape(1, D), w_up, w_up, conv_w, conv_w)
    return _out_linear(hidden, w_down, xs, mod, nctx, 5)


def _ffn_weights(w_up, conv_w, w_down):
    ff = w_down.shape[0]
    ffp = -(-ff // FFN_TILE) * FFN_TILE
    pad = ffp - ff

    def halves(a):
        return jnp.concatenate([jnp.pad(a[:, :ff], ((0, 0), (0, pad))), jnp.pad(a[:, ff:], ((0, 0), (0, pad)))], axis=1)

    return halves(w_up).astype(BF16), halves(conv_w), jnp.pad(w_down, ((0, pad), (0, 0))).astype(BF16)


def _level_tables(C, reverse):
    n_lev = int(np.log2(C))
    t = np.arange(C)
    mats, masks = [], []
    for lev in range(n_lev):
        half = 1 << lev
        parent = t // (2 * half)
        mid = parent * 2 * half + half
        second = (t % (2 * half)) >= half
        a = np.zeros((C, C), np.float32)
        for tt in range(C):
            if second[tt]:
                a[tt, mid[tt]:tt + 1] = 1.0
            else:
                a[tt, tt + 1:mid[tt]] = 1.0
        mats.append(a)
        masks.append(((parent[:, None] == parent[None, :]) & second[:, None] & ~second[None, :]).astype(np.float32))
    mats.append(np.tril(np.ones((C, C), np.float32)))
    mats.append(np.ones((C, C), np.float32))
    mats, masks = np.stack(mats), np.stack(masks)
    if reverse:
        mats, masks = mats[:, ::-1, ::-1], masks[:, ::-1, ::-1]
    return mats.reshape(-1, C), masks


def _chunk_index(i, d, nch, ncc):
    if d == 0:
        return i
    return jnp.where(i < ncc, ncc - 1 - i, nch - 1 - i + ncc)


READOUT_ROWS = 256


def _scan_readout(o_acc, gate_ref, ng_ref, y_ref, n_heads):
    def body(i, carry):
        rows = pl.ds(pl.multiple_of(i * READOUT_ROWS, READOUT_ROWS), READOUT_ROWS)
        for h in range(n_heads):
            sl = slice(h * HEAD_DIM, (h + 1) * HEAD_DIM)
            o = o_acc[rows, sl]
            y = o * lax.rsqrt(jnp.mean(o * o, axis=-1, keepdims=True) + RMS_EPS) * ng_ref[...]
            y_ref[0, rows, sl] = (y * _silu(gate_ref[0, rows, sl])).astype(BF16)
        return carry

    lax.fori_loop(0, o_acc.shape[0] // READOUT_ROWS, body, 0)


def _hgrn2_kernel(q_ref, v_ref, f0_ref, f1_ref, gt_ref, lb_ref, ng_ref, cm_ref, mk_ref, y_ref, o_acc, st_ref,
                  *, S, nctx, hb):
    C = SCAN_CHUNK
    nch, ncc = S // C, nctx // C
    n_lev = mk_ref.shape[1]
    f_refs = (f0_ref, f1_ref)
    st_ref[...] = jnp.zeros(st_ref.shape, F32)

    def zero(i, carry):
        o_acc[pl.ds(pl.multiple_of(i * C, C), C), :] = jnp.zeros((C, o_acc.shape[1]), F32)
        return carry

    lax.fori_loop(0, nch, zero, 0)

    def chunk(i, carry):
        rows = [pl.ds(pl.multiple_of(_chunk_index(i, d, nch, ncc) * C, C), C) for d in range(2)]
        q, k, v, seg = [], [], [], []
        for d in range(2):
            pf = f_refs[d][0, rows[d], :]
            e = jnp.exp(-jnp.abs(pf))
            r = 1.0 / (1.0 + e)
            a = lb_ref[d, 0:1, :]
            b = lb_ref[d, 1:2, :] + (jnp.minimum(pf, 0.0) + jnp.log(r))
            logf = jnp.maximum(a, b) + jnp.log(1.0 + jnp.exp(-jnp.abs(a - b)))
            k.append(lb_ref[d, 2:3, :] * jnp.where(pf >= 0.0, e * r, r))
            q.append(_silu(q_ref[0, rows[d], :]))
            v.append(v_ref[0, rows[d], :])
            seg.append(_dot(cm_ref[d], jnp.concatenate(_split_bf16(logf, 2), axis=0)))
        inst = [(d, h, slice(h * HEAD_DIM, (h + 1) * HEAD_DIM)) for d in range(2) for h in range(hb)]
        qh = [q[d][:, sl] for d, _, sl in inst]
        kh = [k[d][:, sl] for d, _, sl in inst]
        vh = [v[d][:, sl] for d, _, sl in inst]
        sg = [seg[d][:, sl] for d, _, sl in inst]
        vb = [x.astype(BF16) for x in vh]
        st = [st_ref[d, h] for d, h, _ in inst]
        cum = [x[n_lev * C:(n_lev + 1) * C] for x in sg]
        tot = [x[(n_lev + 1) * C:] for x in sg]
        o_inter = [_dot_nt((a * jnp.exp(c)).astype(BF16), s.astype(BF16)) for a, c, s in zip(qh, cum, st)]
        pair = []
        for a, b, x in zip(qh, kh, sg):
            a, b = a.astype(BF16), b.astype(BF16)
            e = [jnp.exp(x[lev * C:(lev + 1) * C]).astype(BF16) for lev in range(n_lev)]
            pair.append([_dot_nt(a * ee, b * ee) for ee in e])
        att = [functools.reduce(jnp.add, [mk_ref[d, lev] * p[lev] for lev in range(n_lev)]) for (d, _, _), p in zip(inst, pair)]
        o_intra = [_dot(a.astype(BF16), x) for a, x in zip(att, vb)]
        upd = [_dot_tn(x, (b * jnp.exp(t - c)).astype(BF16)) for x, b, t, c in zip(vb, kh, tot, cum)]
        for n, (d, h, sl) in enumerate(inst):
            o_acc[rows[d], sl] += o_intra[n] + o_inter[n] + jnp.sum(qh[n] * kh[n], axis=-1, keepdims=True) * vh[n]
            st_ref[d, h] = st[n] * jnp.exp(tot[n][0:1, :]) + upd[n]
        return carry

    lax.fori_loop(0, nch, chunk, 0, unroll=2)

    _scan_readout(o_acc, gt_ref, ng_ref, y_ref, hb)


def _hgrn2_scan(p, lb, norm_g, nctx):
    B, S, W5 = p.shape
    W = W5 // 5
    hb = 2 if W % (2 * HEAD_DIM) == 0 else 1
    wb = hb * HEAD_DIM
    nhg = W // wb
    C = SCAN_CHUNK
    tabs = [_level_tables(C, rev) for rev in (False, True)]
    cm = jnp.asarray(np.stack([np.concatenate([t[0], t[0]], axis=1) for t in tabs]), BF16)
    mk = jnp.asarray(np.stack([t[1] for t in tabs]), F32)
    lbp = jnp.stack([jnp.log(lb), jnp.log1p(-lb), 1.0 - lb], axis=1)
    est = 2 * (5 * S * wb * 4 + S * wb * 2) + S * wb * 4 + 16 * C * wb * 4 * 8
    col = lambda k: pl.BlockSpec((1, S, wb), lambda b, g, k=k: (b, 0, k * nhg + g))
    return pl.pallas_call(
        functools.partial(_hgrn2_kernel, S=S, nctx=nctx, hb=hb),
        grid=(B, nhg),
        in_specs=[
            col(0), col(1), col(2), col(3), col(4),
            pl.BlockSpec((2, 3, wb), lambda b, g: (0, 0, g)),
            pl.BlockSpec((1, HEAD_DIM), lambda b, g: (0, 0)),
            pl.BlockSpec(cm.shape, lambda b, g: (0, 0, 0)),
            pl.BlockSpec(mk.shape, lambda b, g: (0, 0, 0, 0)),
        ],
        out_specs=pl.BlockSpec((1, S, wb), lambda b, g: (b, 0, g)),
        out_shape=jax.ShapeDtypeStruct((B, S, W), BF16),
        scratch_shapes=[pltpu.VMEM((S, wb), F32), pltpu.VMEM((2, hb, HEAD_DIM, HEAD_DIM), F32)],
        compiler_params=_params(("arbitrary", "arbitrary"), est),
        name="hgrn2_scan",
    )(p, p, p, p, p, lbp, norm_g.reshape(1, HEAD_DIM), cm, mk)


GDN_ROWS = 256


def _softplus(x):
    return jnp.maximum(x, 0.0) + jnp.log1p(jnp.exp(-jnp.abs(x)))


def _conv5_silu(x_ref, w_ref, r0, S, nctx):
    n = GDN_ROWS + 16
    lo = pl.multiple_of(jnp.maximum(r0 - 8, 0), 8)
    hi = pl.multiple_of(jnp.minimum(r0 + GDN_ROWS, S - 8), 8)
    keep_prev = jnp.where(jnp.logical_or(r0 == 0, r0 == nctx), 0.0, 1.0)
    keep_next = jnp.where(jnp.logical_or(r0 + GDN_ROWS == nctx, r0 + GDN_ROWS == S), 0.0, 1.0)
    ext = jnp.concatenate([x_ref[0, pl.ds(lo, 8), :] * keep_prev, x_ref[0, pl.ds(r0, GDN_ROWS), :],
                           x_ref[0, pl.ds(hi, 8), :] * keep_next], axis=0)
    half = GDN_CONV // 2
    acc = w_ref[half:half + 1, :] * ext[8:8 + GDN_ROWS]
    for off in range(-half, half + 1):
        if off != 0:
            acc = acc + w_ref[off + half:off + half + 1, :] * pltpu.roll(ext, (-off) % n, 0)[8:8 + GDN_ROWS]
    return _silu(acc)


def _l2norm(x):
    return x * lax.rsqrt(jnp.sum(x * x, axis=-1, keepdims=True) + RMS_EPS)


GDN_GROUP = 4


def _gdn_kernel(pq_ref, pk_ref, pv_ref, pz_ref, bag_ref, bat_ref, cq_ref, ck_ref, cv_ref, gl_ref, gc_ref, ng_ref,
                cmc_ref, cmr_ref, mk_ref, y_ref, qn_ref, kn_ref, vv_ref, o_acc, st_ref, *, S, nctx):
    C, G = SCAN_CHUNK, GDN_GROUP
    nch, ncc = S // C, nctx // C
    ngrp, ncg = nch // G, ncc // G
    n_lev = mk_ref.shape[1] - 2

    def prologue(i, carry):
        r0 = pl.multiple_of(i * GDN_ROWS, GDN_ROWS)
        rows = pl.ds(r0, GDN_ROWS)
        qn_ref[rows, :] = _l2norm(_conv5_silu(pq_ref, cq_ref, r0, S, nctx)) * (HEAD_DIM ** -0.5)
        kn_ref[rows, :] = _l2norm(_conv5_silu(pk_ref, ck_ref, r0, S, nctx))
        vv_ref[rows, :] = _conv5_silu(pv_ref, cv_ref, r0, S, nctx)
        o_acc[rows, :] = jnp.zeros((GDN_ROWS, 2 * HEAD_DIM), F32)
        return carry

    lax.fori_loop(0, S // GDN_ROWS, prologue, 0)
    st_ref[...] = jnp.zeros(st_ref.shape, F32)

    def group_terms(chunks):
        rows = [pl.ds(pl.multiple_of(c * C, C), C) for c, _ in chunks]
        ba = [bag_ref[0, 0, r, :] for r in rows]
        beta8 = [jax.nn.sigmoid(x) for x in ba]
        g8 = [gl_ref[0, 0:1, :] * _softplus(x + gl_ref[0, 1:2, :]) for x in ba]
        g8r = [gc_ref[0, :, 0:1] * _softplus(bat_ref[0, 0, c] + gc_ref[0, :, 1:2]) for c, _ in chunks]
        cum8 = [_dot(cmc_ref[d], jnp.concatenate(_split_bf16(x, 3), axis=0)) for x, (_, d) in zip(g8, chunks)]
        cum8r = [_dot(jnp.concatenate(_split_bf16(x, 3), axis=1), cmr_ref[d]) for x, (_, d) in zip(g8r, chunks)]
        kc = [kn_ref[r, :] for r in rows]
        qc = [qn_ref[r, :] for r in rows]
        kb = [x.astype(BF16) for x in kc]
        kk = [_dot_nt(x, x) for x in kb]
        qk = [_dot_nt(x.astype(BF16), y) for x, y in zip(qc, kb)]
        inst = [(i, h) for i in range(len(chunks)) for h in range(2)]
        beta, gcum, gtot, dec_i, lmat, tinv = [], [], [], [], [], []
        for i, h in inst:
            d = chunks[i][1]
            lb, la = 4 * d + h, 4 * d + 2 + h
            strict, incl = mk_ref[d, n_lev], mk_ref[d, n_lev + 1]
            beta.append(beta8[i][:, lb:lb + 1])
            gcum.append(cum8[i][:C, la:la + 1])
            gtot.append(cum8[i][C:, la:la + 1])
            diff = gcum[-1] - cum8r[i][la:la + 1, :]
            dec_i.append(jnp.exp(jnp.where(incl > 0.5, diff, NEG_BIG)))
            lmat.append(beta[-1] * kk[i] * jnp.exp(jnp.where(strict > 0.5, diff, NEG_BIG)))
            tinv.append((incl - strict) - lmat[-1] * mk_ref[d, 0])
        for lev in range(1, n_lev):
            tb = [t.astype(BF16) for t in tinv]
            x = [_dot(t, (lm * mk_ref[chunks[i][1], lev]).astype(BF16)).astype(BF16) for t, lm, (i, _) in zip(tb, lmat, inst)]
            tinv = [t - _dot(xi, ti) for t, xi, ti in zip(tinv, x, tb)]
        rhs = [jnp.concatenate([vv_ref[rows[i], h * HEAD_DIM:(h + 1) * HEAD_DIM] * b, kc[i] * (b * jnp.exp(g))], axis=1)
               for (i, h), b, g in zip(inst, beta, gcum)]
        sol = [_dot(t.astype(BF16), r.astype(BF16)).astype(BF16) for t, r in zip(tinv, rhs)]
        ku_kw = [_dot_tn((kc[i] * jnp.exp(gt - g)).astype(BF16), s) for (i, _), gt, g, s in zip(inst, gtot, gcum, sol)]
        au_aw = [_dot((qk[i] * dc).astype(BF16), s) for (i, _), dc, s in zip(inst, dec_i, sol)]
        out = {}
        for n, (i, h) in enumerate(inst):
            qt = qc[i] * jnp.exp(gcum[n]) - au_aw[n][:, HEAD_DIM:]
            lhs = jnp.concatenate([ku_kw[n][:, HEAD_DIM:], qt], axis=0).astype(BF16)
            out[i, h] = (lhs, ku_kw[n][:, :HEAD_DIM], au_aw[n][:, :HEAD_DIM], jnp.exp(gtot[n][0:1, :]))
        return out

    def group(gi, carry):
        base = [_chunk_index(gi, d, ngrp, ncg) * G for d in range(2)]
        terms = group_terms([(base[d] + j, d) for d in range(2) for j in range(G)])
        chains = [(d, h) for d in range(2) for h in range(2)]
        st = [st_ref[d, h] for d, h in chains]
        for step in range(G):
            js = [step if d == 0 else G - 1 - step for d, _ in chains]
            r = [_dot(terms[d * G + j, h][0], s.astype(BF16)) for j, (d, h), s in zip(js, chains, st)]
            for n, (j, (d, h)) in enumerate(zip(js, chains)):
                _, ku, au, decay = terms[d * G + j, h]
                rows = pl.ds(pl.multiple_of((base[d] + j) * C, C), C)
                o_acc[rows, h * HEAD_DIM:(h + 1) * HEAD_DIM] += r[n][HEAD_DIM:] + au
                st[n] = decay * st[n] - r[n][:HEAD_DIM] + ku
        for s, (d, h) in zip(st, chains):
            st_ref[d, h] = s
        return carry

    lax.fori_loop(0, ngrp, group, 0)

    _scan_readout(o_acc, pz_ref, ng_ref, y_ref, 2)


def _gdn_scan(p, ba, conv_w, a_log, dt_bias, norm_g, nctx):
    B, S, W6 = p.shape
    W = W6 // 6
    nq = W // HEAD_DIM
    C, G = SCAN_CHUNK, GDN_GROUP
    assert ba.shape[2] == 8 * nq and S % GDN_ROWS == 0 and nctx % GDN_ROWS == 0 and S % (C * G) == 0 and nctx % (C * G) == 0
    n_lev = int(np.log2(C))
    tri, masks = [], []
    for rev in (False, True):
        mats, mk = _level_tables(C, rev)
        tri.append(mats[n_lev * C:])
        strict = mk.sum(0)
        masks.append(np.concatenate([mk, strict[None], (strict + np.eye(C, dtype=np.float32))[None]], axis=0))
    cmc = jnp.asarray(np.stack([np.concatenate([t, t, t], axis=1) for t in tri]), BF16)
    cmr = jnp.asarray(np.stack([np.concatenate([t[:C].T, t[:C].T, t[:C].T], axis=0) for t in tri]), BF16)
    mk = jnp.asarray(np.stack(masks), F32)

    def per_group(a):
        a = a.reshape(a.shape[:-1] + (nq, 2))
        return jnp.moveaxis(a, -2, 0).reshape((nq,) + a.shape[:-4] + (8,))

    bag = jnp.moveaxis(per_group(ba.reshape(B, S, 2, 2, 2 * nq)), 0, 1)
    bat = jnp.swapaxes(bag.reshape(B, nq, S // C, C, 8), 3, 4)
    zeros = jnp.zeros((2, 2 * nq), F32)
    neg_rate = per_group(jnp.stack([zeros, -jnp.exp(a_log.astype(F32))], axis=1))
    dtb = per_group(jnp.stack([zeros, dt_bias.astype(F32)], axis=1))
    gl = jnp.stack([neg_rate, dtb], axis=1)
    q128, q256 = W // HEAD_DIM, W // (2 * HEAD_DIM)
    est = (2 * (2 * S * 128 * 4 + 2 * S * 256 * 4 + S * 128 * 4 + (S // C) * 8 * 128 * 4 + S * 256 * 2)
           + 2 * S * 128 * 4 + 2 * S * 256 * 4 + 16 * (3 * 128 * 128 * 4))
    return pl.pallas_call(
        functools.partial(_gdn_kernel, S=S, nctx=nctx),
        grid=(B, nq),
        in_specs=[
            pl.BlockSpec((1, S, HEAD_DIM), lambda b, g: (b, 0, g)),
            pl.BlockSpec((1, S, HEAD_DIM), lambda b, g: (b, 0, q128 + g)),
            pl.BlockSpec((1, S, 2 * HEAD_DIM), lambda b, g: (b, 0, 2 * q256 + g)),
            pl.BlockSpec((1, S, 2 * HEAD_DIM), lambda b, g: (b, 0, 4 * q256 + g)),
            pl.BlockSpec((1, 1, S, 8), lambda b, g: (b, g, 0, 0)),
            pl.BlockSpec((1, 1, S // C, 8, C), lambda b, g: (b, g, 0, 0, 0)),
            pl.BlockSpec((GDN_CONV, HEAD_DIM), lambda b, g: (0, g)),
            pl.BlockSpec((GDN_CONV, HEAD_DIM), lambda b, g: (0, q128 + g)),
            pl.BlockSpec((GDN_CONV, 2 * HEAD_DIM), lambda b, g: (0, 2 * q256 + g)),
            pl.BlockSpec((1, 2, 8), lambda b, g: (g, 0, 0)),
            pl.BlockSpec((1, 8, 2), lambda b, g: (g, 0, 0)),
            pl.BlockSpec((1, HEAD_DIM), lambda b, g: (0, 0)),
            pl.BlockSpec(cmc.shape, lambda b, g: (0, 0, 0)),
            pl.BlockSpec(cmr.shape, lambda b, g: (0, 0, 0)),
            pl.BlockSpec(mk.shape, lambda b, g: (0, 0, 0, 0)),
        ],
        out_specs=pl.BlockSpec((1, S, 2 * HEAD_DIM), lambda b, g: (b, 0, g)),
        out_shape=jax.ShapeDtypeStruct((B, S, 2 * W), BF16),
        scratch_shapes=[pltpu.VMEM((S, HEAD_DIM), F32), pltpu.VMEM((S, HEAD_DIM), F32), pltpu.VMEM((S, 2 * HEAD_DIM), F32),
                        pltpu.VMEM((S, 2 * HEAD_DIM), F32), pltpu.VMEM((2, 2, HEAD_DIM, HEAD_DIM), F32)],
        compiler_params=_params(("arbitrary", "arbitrary"), est),
        name="gdn_scan",
    )(p, p, p, p, bag, bat, conv_w, conv_w, conv_w, gl, jnp.swapaxes(gl, 1, 2), norm_g.reshape(1, HEAD_DIM), cmc, cmr, mk)


NA_ROWS = 256
NA_PAIRS = NA_WIN_R // 2
NA_UNROLL = 2


def _na_bias_table(rpb):
    q = np.arange(GRID_W)[:, None]
    kc = np.arange(GRID_W)[None, :]
    c0 = np.clip(q - NA_WIN_C // 2, 0, GRID_W - NA_WIN_C)
    valid = (kc >= c0) & (kc < c0 + NA_WIN_C)
    idx = np.clip(kc - q + NA_WIN_C - 1, 0, 2 * NA_WIN_C - 2)
    t = jnp.where(valid, rpb.astype(F32)[:, :, idx], NEG_BIG)
    return jnp.concatenate([t[:, :-1], t[:, 1:]], axis=-1)


def _softmax_pv(scores, values):
    m = functools.reduce(jnp.maximum, [jnp.max(s, axis=-1, keepdims=True) for s in scores])
    ps = [jnp.exp(s - m) for s in scores]
    den = functools.reduce(jnp.add, [jnp.sum(p, axis=-1, keepdims=True) for p in ps])
    num = functools.reduce(jnp.add, [_dot(p.astype(BF16), v) for p, v in zip(ps, values)])
    return num / den


def _na_kernel(q_ref, k_ref, v_ref, qg_ref, kg_ref, gm_ref, hm_ref, t2_ref, o_ref, qn_ref, kn_ref, vb_ref,
               *, S, nctx, grid_rows):
    scale = NA_HEAD_DIM ** -0.5

    def head_rms(x):
        hi, lo = _split_bf16(x * x, 2)
        ms = _dot(hi, gm_ref[...]) + _dot(lo, gm_ref[...])
        return x * lax.rsqrt(ms + RMS_EPS)

    def prologue(i, carry):
        rows = pl.ds(pl.multiple_of(i * NA_ROWS, NA_ROWS), NA_ROWS)
        qn_ref[rows, :] = (head_rms(q_ref[0, rows, :]) * (qg_ref[...] * scale)).astype(BF16)
        kn_ref[rows, :] = (head_rms(k_ref[0, rows, :]) * kg_ref[...]).astype(BF16)
        vb_ref[rows, :] = v_ref[0, rows, :].astype(BF16)
        return carry

    lax.fori_loop(0, S // NA_ROWS, prologue, 0)

    def attend(blocks):
        q4 = [jnp.concatenate([qn_ref[rows, :] * hm_ref[j].astype(BF16) for j in range(NA_GROUP)], axis=0)
              for rows, _ in blocks]
        scores = [[_dot_nt(q, k) if bias is None else _dot_nt(q, k) + bias for k, _, bias in parts]
                  for q, (_, parts) in zip(q4, blocks)]
        for (rows, parts), sc in zip(blocks, scores):
            o4 = _softmax_pv(sc, [v for _, v, _ in parts])
            out = functools.reduce(jnp.add, [o4[j * GRID_W:(j + 1) * GRID_W] * hm_ref[j] for j in range(NA_GROUP)])
            o_ref[0, rows, :] = out.astype(BF16)

    k_ctx, v_ctx = kn_ref[0:nctx, :], vb_ref[0:nctx, :]
    attend([(pl.ds(i * GRID_W, GRID_W), [(k_ctx, v_ctx, None)]) for i in range(nctx // GRID_W)])

    def grid_rows_step(i, carry):
        blocks = []
        for r in (NA_UNROLL * i + u for u in range(NA_UNROLL)):
            r0 = jnp.clip(r - NA_WIN_R // 2, 0, grid_rows - NA_WIN_R)
            dr0 = r0 - r + NA_WIN_R - 1
            rows = pl.ds(pl.multiple_of(nctx + r * GRID_W, GRID_W), GRID_W)
            keys = pl.ds(pl.multiple_of(nctx + r0 * GRID_W, GRID_W), NA_WIN_R * GRID_W)
            bias = jnp.concatenate([t2_ref[0, dr0 + 2 * m] for m in range(NA_PAIRS)], axis=1)
            blocks.append((rows, [(kn_ref[keys, :], vb_ref[keys, :], bias), (kn_ref[0:nctx, :], vb_ref[0:nctx, :], None)]))
        attend(blocks)
        return carry

    lax.fori_loop(0, grid_rows // NA_UNROLL, grid_rows_step, 0)


def _na_attention(p, q_gain, k_gain, rpb, nctx):
    B, S, D3 = p.shape
    D = D3 // 3
    grid_rows = (S - nctx) // GRID_W
    assert (S - nctx) % GRID_W == 0 and grid_rows >= NA_WIN_R and S % NA_ROWS == 0 and nctx % GRID_W == 0
    ng = D // HEAD_DIM
    assert grid_rows % NA_UNROLL == 0
    t2 = _na_bias_table(rpb).reshape(ng, NA_GROUP, 2 * NA_WIN_R - 2, GRID_W, 2 * GRID_W)
    t2 = jnp.swapaxes(t2, 1, 2).reshape(ng, 2 * NA_WIN_R - 2, NA_GROUP * GRID_W, 2 * GRID_W)
    lane_head = np.arange(HEAD_DIM) // NA_HEAD_DIM
    gm = jnp.asarray((lane_head[:, None] == lane_head[None, :]) / NA_HEAD_DIM, BF16)
    hm = jnp.asarray((np.arange(NA_GROUP)[:, None, None] == lane_head[None, None, :]), F32)
    tile = lambda g: jnp.tile(g.astype(F32), NA_GROUP).reshape(1, HEAD_DIM)
    est = (2 * (3 * S * 128 * 4 + t2.shape[1] * t2.shape[2] * 128 * 4 + S * 128 * 2) + 3 * S * 128 * 2
           + 6 * NA_UNROLL * NA_GROUP * GRID_W * (NA_WIN_R * GRID_W + nctx) * 4)
    col = lambda k: pl.BlockSpec((1, S, HEAD_DIM), lambda b, g, k=k: (b, 0, k * ng + g))
    return pl.pallas_call(
        functools.partial(_na_kernel, S=S, nctx=nctx, grid_rows=grid_rows),
        grid=(B, ng),
        in_specs=[
            col(0), col(1), col(2),
            pl.BlockSpec((1, HEAD_DIM), lambda b, g: (0, 0)),
            pl.BlockSpec((1, HEAD_DIM), lambda b, g: (0, 0)),
            pl.BlockSpec(gm.shape, lambda b, g: (0, 0)),
            pl.BlockSpec(hm.shape, lambda b, g: (0, 0, 0)),
            pl.BlockSpec((1,) + t2.shape[1:], lambda b, g: (g, 0, 0, 0)),
        ],
        out_specs=pl.BlockSpec((1, S, HEAD_DIM), lambda b, g: (b, 0, g)),
        out_shape=jax.ShapeDtypeStruct((B, S, D), BF16),
        scratch_shapes=[pltpu.VMEM((S, HEAD_DIM), BF16)] * 3,
        compiler_params=_params(("arbitrary", "arbitrary"), est),
        name="na_attention",
    )(p, p, p, tile(q_gain), tile(k_gain), gm, hm, t2)


def _hgrn2_lower_bound(lb_logits, j):
    cs = jnp.cumsum(jax.nn.softmax(lb_logits.astype(F32), axis=0), axis=0)
    return cs[j] - cs[0]


def kernel(x, c, ctx, c_ctx, ada_w, ada_b, norm_mix_g, norm_ffn_g, hg_w_in, hg_lb_logits, hg_norm_g, hg_w_out,
           gdn_w_in, gdn_conv_w, gdn_a_log, gdn_dt_bias, gdn_norm_g, gdn_w_out, na_w_qkv, na_q_norm_g, na_k_norm_g,
           na_rpb, na_w_out, ffn_w_up, ffn_conv_w, ffn_w_down):
    depth = ada_w.shape[0]
    nctx = ctx.shape[1]
    D = x.shape[2]
    xs = jnp.concatenate([ctx, x], axis=1)
    mods = _ada_params(c, c_ctx, ada_w, ada_b)
    for i in range(depth):
        m, j = i % 3, i // 3
        mod = mods[i]
        if m == 0:
            p = _mod_linear(xs, mod, norm_mix_g[i], hg_w_in[j].astype(BF16), nctx, 0, 1, F32)
            y = _hgrn2_scan(p, _hgrn2_lower_bound(hg_lb_logits, j), hg_norm_g[j], nctx)
            w_out = hg_w_out[j]
        elif m == 1:
            w_in = gdn_w_in[j].astype(BF16)
            p = _mod_linear(xs, mod, norm_mix_g[i], w_in[:, :6 * D], nctx, 0, 1, F32)
            ba = _mod_linear(xs, mod, norm_mix_g[i], w_in[:, 6 * D:], nctx, 0, 1, F32)
            y = _gdn_scan(p, ba, gdn_conv_w[j], gdn_a_log[j], gdn_dt_bias[j], gdn_norm_g[j], nctx)
            w_out = gdn_w_out[j]
        else:
            p = _mod_linear(xs, mod, norm_mix_g[i], na_w_qkv[j].astype(BF16), nctx, 0, 1, F32)
            y = _na_attention(p, na_q_norm_g[j], na_k_norm_g[j], na_rpb[j], nctx)
            w_out = na_w_out[j]
        xs = _out_linear(y, w_out.astype(BF16), xs, mod, nctx, 2)
        w_up, conv_w, w_down = _ffn_weights(ffn_w_up[i], ffn_conv_w[i], ffn_w_down[i])
        xs = _conv_ffn(xs, mod, norm_ffn_g[i], w_up, conv_w, w_down, nctx)
    return xs[:, nctx:]
```

```python
import functools

import numpy as np
import jax
import jax.numpy as jnp
from jax import lax
from jax.experimental import pallas as pl
from jax.experimental.pallas import tpu as pltpu

F32 = jnp.float32
BF16 = jnp.bfloat16

RMS_EPS = 1e-6
N_MOD = 6
HEAD_DIM = 128
SCAN_CHUNK = 64
GRID_W = 64
NA_HEAD_DIM = 32
NA_WIN_R = 8
NA_WIN_C = 16
NA_GROUP = HEAD_DIM // NA_HEAD_DIM
GDN_CONV = 5
FFN_CONV = 3
FFN_TILE = 512
HALO = 16
NEG_BIG = -1e30

V7X_VMEM_BYTES = 64 * 1024 * 1024
VMEM_LIMIT_CAP = 56 * 1024 * 1024

NT_DIMS = (((1,), (1,)), ((), ()))
TN_DIMS = (((0,), (0,)), ((), ()))


def _params(sem, est_bytes, **kw):
    limit = int(min(VMEM_LIMIT_CAP, max(32 * 1024 * 1024, est_bytes * 5 // 4)))
    return pltpu.CompilerParams(dimension_semantics=sem, vmem_limit_bytes=limit, **kw)


def _pick_tile(n, cap, mult):
    best = None
    for t in range(mult, min(n, cap) + 1, mult):
        if n % t == 0:
            best = t
    assert best is not None, (n, cap, mult)
    return best


def _silu(x):
    return x * jax.nn.sigmoid(x)


def _dot(a, b):
    return jnp.dot(a, b, preferred_element_type=F32)


def _dot_nt(a, b):
    return lax.dot_general(a, b, NT_DIMS, preferred_element_type=F32)


def _dot_tn(a, b):
    return lax.dot_general(a, b, TN_DIMS, preferred_element_type=F32)


def _split_bf16(x, n):
    parts = []
    for _ in range(n):
        p = x.astype(BF16)
        parts.append(p)
        x = x - p.astype(F32)
    return parts


def _segment_rows(mod_ref, idx, is_ctx):
    return jnp.where(is_ctx, mod_ref[0, 0, idx:idx + 1, :], mod_ref[0, 1, idx:idx + 1, :])


def _modulate(x, row0, nctx, gain, mod_ref, i_shift, i_scale):
    ms = jnp.mean(x * x, axis=-1, keepdims=True)
    xn = x * lax.rsqrt(ms + RMS_EPS)
    is_ctx = row0 < nctx
    scale = _segment_rows(mod_ref, i_scale, is_ctx)
    shift = _segment_rows(mod_ref, i_shift, is_ctx)
    return xn * (gain * (1.0 + scale)) + shift


MOD_ROWS = 32


def _modulate_tile(h_ref, dst0, x_ref, n, row0, nctx, gain, mod_ref, i_shift, i_scale):
    def body(s, carry):
        r = pl.multiple_of(s * MOD_ROWS, MOD_ROWS)
        h = _modulate(x_ref[0, pl.ds(r, MOD_ROWS), :], row0 + r, nctx, gain, mod_ref, i_shift, i_scale)
        h_ref[pl.ds(dst0 + r, MOD_ROWS), :] = h.astype(BF16)
        return carry

    lax.fori_loop(0, n // MOD_ROWS, body, 0, unroll=4)


def _ada_kernel(c_ref, w_ref, b_ref, o_ref):
    s = _silu(c_ref[...]).astype(BF16)
    o_ref[0] = _dot(s, w_ref[0]) + b_ref[0]


def _ada_params(c, c_ctx, ada_w, ada_b):
    L, D, N = ada_w.shape
    B = c.shape[0]
    rows = -(-(B + 1) // 16) * 16
    cc = jnp.zeros((rows, D), F32).at[:B].set(c).at[B].set(c_ctx)
    tn = _pick_tile(N, 1024, 128)
    est = 2 * (rows * D * 4 + D * tn * 2 + tn * 4 + rows * tn * 4)
    out = pl.pallas_call(
        _ada_kernel,
        grid=(L, N // tn),
        in_specs=[
            pl.BlockSpec((rows, D), lambda l, j: (0, 0)),
            pl.BlockSpec((1, D, tn), lambda l, j: (l, 0, j)),
            pl.BlockSpec((1, 1, tn), lambda l, j: (l, 0, j)),
        ],
        out_specs=pl.BlockSpec((1, rows, tn), lambda l, j: (l, 0, j)),
        out_shape=jax.ShapeDtypeStruct((L, rows, N), F32),
        compiler_params=_params(("arbitrary", "arbitrary"), est),
        name="ada_ln",
    )(cc, ada_w.astype(BF16), ada_b.reshape(L, 1, N))
    lat = out[:, :B].reshape(L, B, 1, N_MOD, D)
    ctx = jnp.broadcast_to(out[:, B].reshape(L, 1, 1, N_MOD, D), (L, B, 1, N_MOD, D))
    return jnp.concatenate([ctx, lat], axis=2)


def _mod_linear_kernel(x_ref, mod_ref, g_ref, w_ref, o_ref, h_ref, *, tm, nctx, i_shift, i_scale):
    @pl.when(pl.program_id(2) == 0)
    def _():
        _modulate_tile(h_ref, 0, x_ref, tm, pl.program_id(1) * tm, nctx, g_ref[...], mod_ref, i_shift, i_scale)

    o_ref[0] = _dot(h_ref[...], w_ref[...]).astype(o_ref.dtype)


def _mod_linear(xs, mod, gain, w, nctx, i_shift, i_scale, out_dtype):
    B, S, D = xs.shape
    N = w.shape[1]
    tm = _pick_tile(S, 768, 16)
    tn = N if N < 128 else _pick_tile(N, 1024, 128)
    assert tm % MOD_ROWS == 0 and nctx % MOD_ROWS == 0
    osz = jnp.dtype(out_dtype).itemsize
    est = 2 * (tm * D * 4 + 2 * N_MOD * D * 4 + D * tn * 2 + tm * tn * osz) + tm * D * 2 + 3 * tm * D * 4
    return pl.pallas_call(
        functools.partial(_mod_linear_kernel, tm=tm, nctx=nctx, i_shift=i_shift, i_scale=i_scale),
        grid=(B, S // tm, N // tn),
        in_specs=[
            pl.BlockSpec((1, tm, D), lambda b, i, j: (b, i, 0)),
            pl.BlockSpec((1, 2, N_MOD, D), lambda b, i, j: (b, 0, 0, 0)),
            pl.BlockSpec((1, D), lambda b, i, j: (0, 0)),
            pl.BlockSpec((D, tn), lambda b, i, j: (0, j)),
        ],
        out_specs=pl.BlockSpec((1, tm, tn), lambda b, i, j: (b, i, j)),
        out_shape=jax.ShapeDtypeStruct((B, S, N), out_dtype),
        scratch_shapes=[pltpu.VMEM((tm, D), BF16)],
        compiler_params=_params(("arbitrary", "arbitrary", "arbitrary"), est),
        name="mod_linear",
    )(xs, mod, gain.reshape(1, D), w)


def _out_linear_kernel(y_ref, w_ref, x_ref, mod_ref, o_ref, *, tm, nctx, i_gate):
    t = pl.program_id(1) * tm + lax.broadcasted_iota(jnp.int32, (tm, 1), 0)
    gate = _segment_rows(mod_ref, i_gate, t < nctx)
    o_ref[0] = x_ref[0] + gate * _dot(y_ref[0], w_ref[...])


def _out_linear(y, w, xs, mod, nctx, i_gate):
    B, S, K = y.shape
    D = w.shape[1]
    tm = _pick_tile(S, 768, 16)
    tn = _pick_tile(D, 1024 if K <= 2048 else 512, 128)
    est = 2 * (tm * K * 2 + K * tn * 2 + 2 * tm * tn * 4 + 2 * N_MOD * tn * 4) + 2 * tm * tn * 4
    return pl.pallas_call(
        functools.partial(_out_linear_kernel, tm=tm, nctx=nctx, i_gate=i_gate),
        grid=(B, S // tm, D // tn),
        in_specs=[
            pl.BlockSpec((1, tm, K), lambda b, i, j: (b, i, 0)),
            pl.BlockSpec((K, tn), lambda b, i, j: (0, j)),
            pl.BlockSpec((1, tm, tn), lambda b, i, j: (b, i, j)),
            pl.BlockSpec((1, 2, N_MOD, tn), lambda b, i, j: (b, 0, 0, j)),
        ],
        out_specs=pl.BlockSpec((1, tm, tn), lambda b, i, j: (b, i, j)),
        out_shape=jax.ShapeDtypeStruct((B, S, D), F32),
        compiler_params=_params(("arbitrary", "arbitrary", "arbitrary"), est),
        name="out_linear",
    )(y, w, xs, mod)


def _ffn_up_kernel(xp_ref, x_ref, xn_ref, mod_ref, g_ref, wa_ref, wb_ref, ca_ref, cb_ref, o_ref, h_ref, *, tm, nctx, S):
    i, f = pl.program_id(1), pl.program_id(2)
    row0 = i * tm
    gain = g_ref[...]

    @pl.when(f == 0)
    def _():
        h_ref[0:HALO] = _modulate(xp_ref[0], row0 - HALO, nctx, gain, mod_ref, 3, 4).astype(BF16)
        _modulate_tile(h_ref, HALO, x_ref, tm, row0, nctx, gain, mod_ref, 3, 4)
        h_ref[HALO + tm:] = _modulate(xn_ref[0], row0 + tm, nctx, gain, mod_ref, 3, 4).astype(BF16)

    t = row0 + lax.broadcasted_iota(jnp.int32, (tm, 1), 0)
    has_prev = jnp.logical_and(t != 0, t != nctx).astype(F32)
    has_next = jnp.logical_and(t != nctx - 1, t != S - 1).astype(F32)
    rows = tm + 2 * HALO

    def conv(w_ref, c_ref):
        u = _dot(h_ref[...], w_ref[...])
        prev = pltpu.roll(u, 1, 0)[HALO:HALO + tm] * has_prev
        nxt = pltpu.roll(u, rows - 1, 0)[HALO:HALO + tm] * has_next
        return c_ref[0:1, :] * prev + c_ref[1:2, :] * u[HALO:HALO + tm] + c_ref[2:3, :] * nxt

    o_ref[0] = (_silu(conv(wa_ref, ca_ref)) * conv(wb_ref, cb_ref)).astype(BF16)


def _conv_ffn(xs, mod, gain, w_up, conv_w, w_down, nctx):
    B, S, D = xs.shape
    ffp = w_down.shape[0]
    tf = FFN_TILE
    nf = ffp // tf
    tm = _pick_tile(S, 768, HALO)
    assert tm % MOD_ROWS == 0 and nctx % MOD_ROWS == 0 and MOD_ROWS % HALO == 0
    nh = tm // HALO
    last = S // HALO - 1
    est = (2 * (tm * D * 4 + 2 * HALO * D * 4 + 2 * N_MOD * D * 4 + 2 * D * tf * 2 + 2 * 8 * tf * 4 + tm * tf * 2)
           + (tm + 2 * HALO) * D * 2 + 10 * (tm + 2 * HALO) * tf * 4)
    hidden = pl.pallas_call(
        functools.partial(_ffn_up_kernel, tm=tm, nctx=nctx, S=S),
        grid=(B, S // tm, nf),
        in_specs=[
            pl.BlockSpec((1, HALO, D), lambda b, i, f: (b, jnp.maximum(i * nh - 1, 0), 0)),
            pl.BlockSpec((1, tm, D), lambda b, i, f: (b, i, 0)),
            pl.BlockSpec((1, HALO, D), lambda b, i, f: (b, jnp.minimum((i + 1) * nh, last), 0)),
            pl.BlockSpec((1, 2, N_MOD, D), lambda b, i, f: (b, 0, 0, 0)),
            pl.BlockSpec((1, D), lambda b, i, f: (0, 0)),
            pl.BlockSpec((D, tf), lambda b, i, f: (0, f)),
            pl.BlockSpec((D, tf), lambda b, i, f: (0, nf + f)),
            pl.BlockSpec((FFN_CONV, tf), lambda b, i, f: (0, f)),
            pl.BlockSpec((FFN_CONV, tf), lambda b, i, f: (0, nf + f)),
        ],
        out_specs=pl.BlockSpec((1, tm, tf), lambda b, i, f: (b, i, f)),
        out_shape=jax.ShapeDtypeStruct((B, S, ffp), BF16),
        scratch_shapes=[pltpu.VMEM((tm + 2 * HALO, D), BF16)],
        compiler_params=_params(("arbitrary", "arbitrary", "arbitrary"), est),
        name="ffn_up",
    )(xs, xs, xs, mod, gain.reshape(1, D), w_up, w_up, conv_w, conv_w)
    return _out_linear(hidden, w_down, xs, mod, nctx, 5)


def _ffn_weights(w_up, conv_w, w_down):
    ff = w_down.shape[0]
    ffp = -(-ff // FFN_TILE) * FFN_TILE
    pad = ffp - ff

    def halves(a):
        return jnp.concatenate([jnp.pad(a[:, :ff], ((0, 0), (0, pad))), jnp.pad(a[:, ff:], ((0, 0), (0, pad)))], axis=1)

    return halves(w_up).astype(BF16), halves(conv_w), jnp.pad(w_down, ((0, pad), (0, 0))).astype(BF16)


def _level_tables(C, reverse):
    n_lev = int(np.log2(C))
    t = np.arange(C)
    mats, masks = [], []
    for lev in range(n_lev):
        half = 1 << lev
        parent = t // (2 * half)
        mid = parent * 2 * half + half
        second = (t % (2 * half)) >= half
        a = np.zeros((C, C), np.float32)
        for tt in range(C):
            if second[tt]:
                a[tt, mid[tt]:tt + 1] = 1.0
            else:
                a[tt, tt + 1:mid[tt]] = 1.0
        mats.append(a)
        masks.append(((parent[:, None] == parent[None, :]) & second[:, None] & ~second[None, :]).astype(np.float32))
    mats.append(np.tril(np.ones((C, C), np.float32)))
    mats.append(np.ones((C, C), np.float32))
    mats, masks = np.stack(mats), np.stack(masks)
    if reverse:
        mats, masks = mats[:, ::-1, ::-1], masks[:, ::-1, ::-1]
    return mats.reshape(-1, C), masks


def _chunk_index(i, d, nch, ncc):
    if d == 0:
        return i
    return jnp.where(i < ncc, ncc - 1 - i, nch - 1 - i + ncc)


READOUT_ROWS = 256
HG_GROUP = 2


def _scan_readout(o_acc, gate_ref, ng_ref, y_ref, n_heads):
    def body(i, carry):
        rows = pl.ds(pl.multiple_of(i * READOUT_ROWS, READOUT_ROWS), READOUT_ROWS)
        for h in range(n_heads):
            sl = slice(h * HEAD_DIM, (h + 1) * HEAD_DIM)
            o = o_acc[rows, sl]
            y = o * lax.rsqrt(jnp.mean(o * o, axis=-1, keepdims=True) + RMS_EPS) * ng_ref[...]
            y_ref[0, rows, sl] = (y * _silu(gate_ref[0, rows, sl])).astype(BF16)
        return carry

    lax.fori_loop(0, o_acc.shape[0] // READOUT_ROWS, body, 0)


def _hgrn2_kernel(q_ref, v_ref, f0_ref, f1_ref, gt_ref, lb_ref, ng_ref, cm_ref, mk_ref, y_ref, o_acc, st_ref,
                  *, S, nctx, hb):
    C = SCAN_CHUNK
    nch, ncc = S // C, nctx // C
    n_lev = mk_ref.shape[1]
    f_refs = (f0_ref, f1_ref)
    st_ref[...] = jnp.zeros(st_ref.shape, F32)

    def zero(i, carry):
        o_acc[pl.ds(pl.multiple_of(i * C, C), C), :] = jnp.zeros((C, o_acc.shape[1]), F32)
        return carry

    lax.fori_loop(0, nch, zero, 0)

    def chunk(i, carry):
        steps = [(j, d) for j in range(HG_GROUP) for d in range(2)]
        rows = [pl.ds(pl.multiple_of(_chunk_index(i * HG_GROUP + j, d, nch, ncc) * C, C), C) for j, d in steps]
        q, k, v, seg = [], [], [], []
        for (j, d), rw in zip(steps, rows):
            pf = f_refs[d][0, rw, :]
            e = jnp.exp(-jnp.abs(pf))
            r = 1.0 / (1.0 + e)
            a = lb_ref[d, 0:1, :]
            b = lb_ref[d, 1:2, :] + (jnp.minimum(pf, 0.0) + jnp.log(r))
            logf = jnp.maximum(a, b) + jnp.log(1.0 + jnp.exp(-jnp.abs(a - b)))
            k.append(lb_ref[d, 2:3, :] * jnp.where(pf >= 0.0, e * r, r))
            q.append(_silu(q_ref[0, rw, :]))
            v.append(v_ref[0, rw, :])
            seg.append(_dot(cm_ref[d], jnp.concatenate(_split_bf16(logf, 2), axis=0)))
        inst = [(n, d, h, slice(h * HEAD_DIM, (h + 1) * HEAD_DIM)) for n, (_, d) in enumerate(steps) for h in range(hb)]
        qh = [q[n][:, sl] for n, _, _, sl in inst]
        kh = [k[n][:, sl] for n, _, _, sl in inst]
        vh = [v[n][:, sl] for n, _, _, sl in inst]
        sg = [seg[n][:, sl] for n, _, _, sl in inst]
        vb = [x.astype(BF16) for x in vh]
        cum = [x[n_lev * C:(n_lev + 1) * C] for x in sg]
        tot = [x[(n_lev + 1) * C:] for x in sg]
        pair = []
        for a, b, x in zip(qh, kh, sg):
            a, b = a.astype(BF16), b.astype(BF16)
            e = [jnp.exp(x[lev * C:(lev + 1) * C]).astype(BF16) for lev in range(n_lev)]
            pair.append([_dot_nt(a * ee, b * ee) for ee in e])
        upd = [_dot_tn(x, (b * jnp.exp(t - c)).astype(BF16)) for x, b, t, c in zip(vb, kh, tot, cum)]
        st, cur = [], {}
        for m, (n, d, h, _) in enumerate(inst):
            s = cur[d, h] if (d, h) in cur else st_ref[d, h]
            st.append(s)
            cur[d, h] = s * jnp.exp(tot[m][0:1, :]) + upd[m]
        o_inter = [_dot_nt((a * jnp.exp(c)).astype(BF16), s.astype(BF16)) for a, c, s in zip(qh, cum, st)]
        att = [functools.reduce(jnp.add, [mk_ref[d, lev] * p[lev] for lev in range(n_lev)]) for (_, d, _, _), p in zip(inst, pair)]
        o_intra = [_dot(a.astype(BF16), x) for a, x in zip(att, vb)]
        for m, (n, d, h, sl) in enumerate(inst):
            o_acc[rows[n], sl] += o_intra[m] + o_inter[m] + jnp.sum(qh[m] * kh[m], axis=-1, keepdims=True) * vh[m]
        for (d, h), s in cur.items():
            st_ref[d, h] = s
        return carry

    lax.fori_loop(0, nch // HG_GROUP, chunk, 0)

    _scan_readout(o_acc, gt_ref, ng_ref, y_ref, hb)


def _hgrn2_scan(p, lb, norm_g, nctx):
    B, S, W5 = p.shape
    W = W5 // 5
    hb = 2 if W % (2 * HEAD_DIM) == 0 else 1
    wb = hb * HEAD_DIM
    nhg = W // wb
    C = SCAN_CHUNK
    tabs = [_level_tables(C, rev) for rev in (False, True)]
    cm = jnp.asarray(np.stack([np.concatenate([t[0], t[0]], axis=1) for t in tabs]), BF16)
    mk = jnp.asarray(np.stack([t[1] for t in tabs]), F32)
    lbp = jnp.stack([jnp.log(lb), jnp.log1p(-lb), 1.0 - lb], axis=1)
    est = 2 * (5 * S * wb * 4 + S * wb * 2) + S * wb * 4 + 16 * C * wb * 4 * 8
    col = lambda k: pl.BlockSpec((1, S, wb), lambda b, g, k=k: (b, 0, k * nhg + g))
    return pl.pallas_call(
        functools.partial(_hgrn2_kernel, S=S, nctx=nctx, hb=hb),
        grid=(B, nhg),
        in_specs=[
            col(0), col(1), col(2), col(3), col(4),
            pl.BlockSpec((2, 3, wb), lambda b, g: (0, 0, g)),
            pl.BlockSpec((1, HEAD_DIM), lambda b, g: (0, 0)),
            pl.BlockSpec(cm.shape, lambda b, g: (0, 0, 0)),
            pl.BlockSpec(mk.shape, lambda b, g: (0, 0, 0, 0)),
        ],
        out_specs=pl.BlockSpec((1, S, wb), lambda b, g: (b, 0, g)),
        out_shape=jax.ShapeDtypeStruct((B, S, W), BF16),
        scratch_shapes=[pltpu.VMEM((S, wb), F32), pltpu.VMEM((2, hb, HEAD_DIM, HEAD_DIM), F32)],
        compiler_params=_params(("arbitrary", "arbitrary"), est),
        name="hgrn2_scan",
    )(p, p, p, p, p, lbp, norm_g.reshape(1, HEAD_DIM), cm, mk)


GDN_ROWS = 256


def _softplus(x):
    return jnp.maximum(x, 0.0) + jnp.log1p(jnp.exp(-jnp.abs(x)))


def _conv5_silu(x_ref, w_ref, r0, S, nctx):
    n = GDN_ROWS + 16
    lo = pl.multiple_of(jnp.maximum(r0 - 8, 0), 8)
    hi = pl.multiple_of(jnp.minimum(r0 + GDN_ROWS, S - 8), 8)
    keep_prev = jnp.where(jnp.logical_or(r0 == 0, r0 == nctx), 0.0, 1.0)
    keep_next = jnp.where(jnp.logical_or(r0 + GDN_ROWS == nctx, r0 + GDN_ROWS == S), 0.0, 1.0)
    ext = jnp.concatenate([x_ref[0, pl.ds(lo, 8), :] * keep_prev, x_ref[0, pl.ds(r0, GDN_ROWS), :],
                           x_ref[0, pl.ds(hi, 8), :] * keep_next], axis=0)
    half = GDN_CONV // 2
    acc = w_ref[half:half + 1, :] * ext[8:8 + GDN_ROWS]
    for off in range(-half, half + 1):
        if off != 0:
            acc = acc + w_ref[off + half:off + half + 1, :] * pltpu.roll(ext, (-off) % n, 0)[8:8 + GDN_ROWS]
    return _silu(acc)


def _l2norm(x):
    return x * lax.rsqrt(jnp.sum(x * x, axis=-1, keepdims=True) + RMS_EPS)


GDN_GROUP = 4


def _gdn_kernel(pq_ref, pk_ref, pv_ref, pz_ref, bag_ref, bat_ref, cq_ref, ck_ref, cv_ref, gl_ref, gc_ref, ng_ref,
                cmc_ref, cmr_ref, mk_ref, y_ref, qn_ref, kn_ref, vv_ref, o_acc, st_ref, lhs_s, ku_s, au_s, dc_s, *, S, nctx):
    C, G = SCAN_CHUNK, GDN_GROUP
    nch, ncc = S // C, nctx // C
    ngrp, ncg = nch // G, ncc // G
    n_lev = mk_ref.shape[1] - 2

    def prologue(i, carry):
        r0 = pl.multiple_of(i * GDN_ROWS, GDN_ROWS)
        rows = pl.ds(r0, GDN_ROWS)
        qn_ref[rows, :] = _l2norm(_conv5_silu(pq_ref, cq_ref, r0, S, nctx)) * (HEAD_DIM ** -0.5)
        kn_ref[rows, :] = _l2norm(_conv5_silu(pk_ref, ck_ref, r0, S, nctx))
        vv_ref[rows, :] = _conv5_silu(pv_ref, cv_ref, r0, S, nctx)
        o_acc[rows, :] = jnp.zeros((GDN_ROWS, 2 * HEAD_DIM), F32)
        return carry

    lax.fori_loop(0, S // GDN_ROWS, prologue, 0)
    st_ref[...] = jnp.zeros(st_ref.shape, F32)

    def group_terms(chunks, between):
        rows = [pl.ds(pl.multiple_of(c * C, C), C) for c, _ in chunks]
        ba = [bag_ref[0, 0, r, :] for r in rows]
        beta8 = [jax.nn.sigmoid(x) for x in ba]
        g8 = [gl_ref[0, 0:1, :] * _softplus(x + gl_ref[0, 1:2, :]) for x in ba]
        g8r = [gc_ref[0, :, 0:1] * _softplus(bat_ref[0, 0, c] + gc_ref[0, :, 1:2]) for c, _ in chunks]
        cum8 = [_dot(cmc_ref[d], jnp.concatenate(_split_bf16(x, 3), axis=0)) for x, (_, d) in zip(g8, chunks)]
        cum8r = [_dot(jnp.concatenate(_split_bf16(x, 3), axis=1), cmr_ref[d]) for x, (_, d) in zip(g8r, chunks)]
        kc = [kn_ref[r, :] for r in rows]
        qc = [qn_ref[r, :] for r in rows]
        kb = [x.astype(BF16) for x in kc]
        kk = [_dot_nt(x, x) for x in kb]
        qk = [_dot_nt(x.astype(BF16), y) for x, y in zip(qc, kb)]
        inst = [(i, h) for i in range(len(chunks)) for h in range(2)]
        beta, gcum, gtot, dec_i, lmat, tinv = [], [], [], [], [], []
        for i, h in inst:
            d = chunks[i][1]
            lb, la = 4 * d + h, 4 * d + 2 + h
            strict, incl = mk_ref[d, n_lev], mk_ref[d, n_lev + 1]
            beta.append(beta8[i][:, lb:lb + 1])
            gcum.append(cum8[i][:C, la:la + 1])
            gtot.append(cum8[i][C:, la:la + 1])
            diff = gcum[-1] - cum8r[i][la:la + 1, :]
            dec_i.append(jnp.exp(jnp.where(incl > 0.5, diff, NEG_BIG)))
            lmat.append(beta[-1] * kk[i] * jnp.exp(jnp.where(strict > 0.5, diff, NEG_BIG)))
            tinv.append((incl - strict) - lmat[-1] * mk_ref[d, 0])
        for lev in range(1, n_lev):
            tb = [t.astype(BF16) for t in tinv]
            x = [_dot(t, (lm * mk_ref[chunks[i][1], lev]).astype(BF16)).astype(BF16) for t, lm, (i, _) in zip(tb, lmat, inst)]
            tinv = [t - _dot(xi, ti) for t, xi, ti in zip(tinv, x, tb)]
            if lev <= len(between):
                between[lev - 1]()
        rhs = [jnp.concatenate([vv_ref[rows[i], h * HEAD_DIM:(h + 1) * HEAD_DIM] * b, kc[i] * (b * jnp.exp(g))], axis=1)
               for (i, h), b, g in zip(inst, beta, gcum)]
        sol = [_dot(t.astype(BF16), r.astype(BF16)).astype(BF16) for t, r in zip(tinv, rhs)]
        ku_kw = [_dot_tn((kc[i] * jnp.exp(gt - g)).astype(BF16), s) for (i, _), gt, g, s in zip(inst, gtot, gcum, sol)]
        au_aw = [_dot((qk[i] * dc).astype(BF16), s) for (i, _), dc, s in zip(inst, dec_i, sol)]
        for n, (i, h) in enumerate(inst):
            qt = qc[i] * jnp.exp(gcum[n]) - au_aw[n][:, HEAD_DIM:]
            lhs_s[2 * i + h] = jnp.concatenate([ku_kw[n][:, HEAD_DIM:], qt], axis=0).astype(BF16)
            ku_s[2 * i + h] = ku_kw[n][:, :HEAD_DIM]
            au_s[2 * i + h] = au_aw[n][:, :HEAD_DIM]
            dc_s[2 * i + h] = jnp.broadcast_to(jnp.exp(gtot[n][0:1, :]), (8, HEAD_DIM))

    chains = [(d, h) for d in range(2) for h in range(2)]

    def group_chunks(gi):
        base = [_chunk_index(gi, d, ngrp, ncg) * G for d in range(2)]
        return base, [(base[d] + j, d) for d in range(2) for j in range(G)]

    def state_step(gi, step):
        base, _ = group_chunks(gi)
        js = [step if d == 0 else G - 1 - step for d, _ in chains]
        slots = [(d * G + j) * 2 + h for j, (d, h) in zip(js, chains)]
        st = [st_ref[d, h] for d, h in chains]
        r = [_dot(lhs_s[n], s.astype(BF16)) for n, s in zip(slots, st)]
        for n, j, (d, h), s, rr in zip(slots, js, chains, st, r):
            rows = pl.ds(pl.multiple_of((base[d] + j) * C, C), C)
            o_acc[rows, h * HEAD_DIM:(h + 1) * HEAD_DIM] += rr[HEAD_DIM:] + au_s[n]
            st_ref[d, h] = dc_s[n, 0:1, :] * s - rr[:HEAD_DIM] + ku_s[n]

    def group(gi, carry):
        group_terms(group_chunks(gi)[1], [functools.partial(state_step, gi - 1, step) for step in range(G)])
        return carry

    group_terms(group_chunks(0)[1], [])
    lax.fori_loop(1, ngrp, group, 0)
    for step in range(G):
        state_step(ngrp - 1, step)

    _scan_readout(o_acc, pz_ref, ng_ref, y_ref, 2)


def _gdn_scan(p, ba, conv_w, a_log, dt_bias, norm_g, nctx):
    B, S, W6 = p.shape
    W = W6 // 6
    nq = W // HEAD_DIM
    C, G = SCAN_CHUNK, GDN_GROUP
    assert ba.shape[2] == 8 * nq and S % GDN_ROWS == 0 and nctx % GDN_ROWS == 0 and S % (C * G) == 0 and nctx % (C * G) == 0
    n_lev = int(np.log2(C))
    tri, masks = [], []
    for rev in (False, True):
        mats, mk = _level_tables(C, rev)
        tri.append(mats[n_lev * C:])
        strict = mk.sum(0)
        masks.append(np.concatenate([mk, strict[None], (strict + np.eye(C, dtype=np.float32))[None]], axis=0))
    cmc = jnp.asarray(np.stack([np.concatenate([t, t, t], axis=1) for t in tri]), BF16)
    cmr = jnp.asarray(np.stack([np.concatenate([t[:C].T, t[:C].T, t[:C].T], axis=0) for t in tri]), BF16)
    mk = jnp.asarray(np.stack(masks), F32)

    def per_group(a):
        a = a.reshape(a.shape[:-1] + (nq, 2))
        return jnp.moveaxis(a, -2, 0).reshape((nq,) + a.shape[:-4] + (8,))

    bag = jnp.moveaxis(per_group(ba.reshape(B, S, 2, 2, 2 * nq)), 0, 1)
    bat = jnp.swapaxes(bag.reshape(B, nq, S // C, C, 8), 3, 4)
    zeros = jnp.zeros((2, 2 * nq), F32)
    neg_rate = per_group(jnp.stack([zeros, -jnp.exp(a_log.astype(F32))], axis=1))
    dtb = per_group(jnp.stack([zeros, dt_bias.astype(F32)], axis=1))
    gl = jnp.stack([neg_rate, dtb], axis=1)
    q128, q256 = W // HEAD_DIM, W // (2 * HEAD_DIM)
    est = (2 * (2 * S * 128 * 4 + 2 * S * 256 * 4 + S * 128 * 4 + (S // C) * 8 * 128 * 4 + S * 256 * 2)
           + 2 * S * 128 * 4 + 2 * S * 256 * 4 + 16 * (3 * 128 * 128 * 4))
    return pl.pallas_call(
        functools.partial(_gdn_kernel, S=S, nctx=nctx),
        grid=(B, nq),
        in_specs=[
            pl.BlockSpec((1, S, HEAD_DIM), lambda b, g: (b, 0, g)),
            pl.BlockSpec((1, S, HEAD_DIM), lambda b, g: (b, 0, q128 + g)),
            pl.BlockSpec((1, S, 2 * HEAD_DIM), lambda b, g: (b, 0, 2 * q256 + g)),
            pl.BlockSpec((1, S, 2 * HEAD_DIM), lambda b, g: (b, 0, 4 * q256 + g)),
            pl.BlockSpec((1, 1, S, 8), lambda b, g: (b, g, 0, 0)),
            pl.BlockSpec((1, 1, S // C, 8, C), lambda b, g: (b, g, 0, 0, 0)),
            pl.BlockSpec((GDN_CONV, HEAD_DIM), lambda b, g: (0, g)),
            pl.BlockSpec((GDN_CONV, HEAD_DIM), lambda b, g: (0, q128 + g)),
            pl.BlockSpec((GDN_CONV, 2 * HEAD_DIM), lambda b, g: (0, 2 * q256 + g)),
            pl.BlockSpec((1, 2, 8), lambda b, g: (g, 0, 0)),
            pl.BlockSpec((1, 8, 2), lambda b, g: (g, 0, 0)),
            pl.BlockSpec((1, HEAD_DIM), lambda b, g: (0, 0)),
            pl.BlockSpec(cmc.shape, lambda b, g: (0, 0, 0)),
            pl.BlockSpec(cmr.shape, lambda b, g: (0, 0, 0)),
            pl.BlockSpec(mk.shape, lambda b, g: (0, 0, 0, 0)),
        ],
        out_specs=pl.BlockSpec((1, S, 2 * HEAD_DIM), lambda b, g: (b, 0, g)),
        out_shape=jax.ShapeDtypeStruct((B, S, 2 * W), BF16),
        scratch_shapes=[pltpu.VMEM((S, HEAD_DIM), F32), pltpu.VMEM((S, HEAD_DIM), F32), pltpu.VMEM((S, 2 * HEAD_DIM), F32),
                        pltpu.VMEM((S, 2 * HEAD_DIM), F32), pltpu.VMEM((2, 2, HEAD_DIM, HEAD_DIM), F32),
                        pltpu.VMEM((4 * G, HEAD_DIM + C, HEAD_DIM), BF16), pltpu.VMEM((4 * G, HEAD_DIM, HEAD_DIM), F32),
                        pltpu.VMEM((4 * G, C, HEAD_DIM), F32), pltpu.VMEM((4 * G, 8, HEAD_DIM), F32)],
        compiler_params=_params(("arbitrary", "arbitrary"), est),
        name="gdn_scan",
    )(p, p, p, p, bag, bat, conv_w, conv_w, conv_w, gl, jnp.swapaxes(gl, 1, 2), norm_g.reshape(1, HEAD_DIM), cmc, cmr, mk)


NA_ROWS = 256
NA_PAIRS = NA_WIN_R // 2
NA_UNROLL = 4


def _na_bias_table(rpb):
    q = np.arange(GRID_W)[:, None]
    kc = np.arange(GRID_W)[None, :]
    c0 = np.clip(q - NA_WIN_C // 2, 0, GRID_W - NA_WIN_C)
    valid = (kc >= c0) & (kc < c0 + NA_WIN_C)
    idx = np.clip(kc - q + NA_WIN_C - 1, 0, 2 * NA_WIN_C - 2)
    t = jnp.where(valid, rpb.astype(F32)[:, :, idx], NEG_BIG)
    return jnp.concatenate([t[:, :-1], t[:, 1:]], axis=-1)


def _softmax_pv(scores, values):
    m = functools.reduce(jnp.maximum, [jnp.max(s, axis=-1, keepdims=True) for s in scores])
    ps = [jnp.exp(s - m) for s in scores]
    den = functools.reduce(jnp.add, [jnp.sum(p, axis=-1, keepdims=True) for p in ps])
    num = functools.reduce(jnp.add, [_dot(p.astype(BF16), v) for p, v in zip(ps, values)])
    return num / den


def _na_kernel(q_ref, k_ref, v_ref, qg_ref, kg_ref, gm_ref, hm_ref, t2_ref, o_ref, qn_ref, kn_ref, vb_ref,
               *, S, nctx, grid_rows):
    scale = NA_HEAD_DIM ** -0.5

    def head_rms(x):
        hi, lo = _split_bf16(x * x, 2)
        ms = _dot(hi, gm_ref[...]) + _dot(lo, gm_ref[...])
        return x * lax.rsqrt(ms + RMS_EPS)

    def prologue(i, carry):
        rows = pl.ds(pl.multiple_of(i * NA_ROWS, NA_ROWS), NA_ROWS)
        qn_ref[rows, :] = (head_rms(q_ref[0, rows, :]) * (qg_ref[...] * scale)).astype(BF16)
        kn_ref[rows, :] = (head_rms(k_ref[0, rows, :]) * kg_ref[...]).astype(BF16)
        vb_ref[rows, :] = v_ref[0, rows, :].astype(BF16)
        return carry

    lax.fori_loop(0, S // NA_ROWS, prologue, 0)

    def attend(blocks):
        q4 = [jnp.concatenate([qn_ref[rows, :] * hm_ref[j].astype(BF16) for j in range(NA_GROUP)], axis=0)
              for rows, _ in blocks]
        scores = [[_dot_nt(q, k) if bias is None else _dot_nt(q, k) + bias for k, _, bias in parts]
                  for q, (_, parts) in zip(q4, blocks)]
        for (rows, parts), sc in zip(blocks, scores):
            o4 = _softmax_pv(sc, [v for _, v, _ in parts])
            out = functools.reduce(jnp.add, [o4[j * GRID_W:(j + 1) * GRID_W] * hm_ref[j] for j in range(NA_GROUP)])
            o_ref[0, rows, :] = out.astype(BF16)

    k_ctx, v_ctx = kn_ref[0:nctx, :], vb_ref[0:nctx, :]
    attend([(pl.ds(i * GRID_W, GRID_W), [(k_ctx, v_ctx, None)]) for i in range(nctx // GRID_W)])

    def grid_rows_step(i, carry):
        blocks = []
        for r in (NA_UNROLL * i + u for u in range(NA_UNROLL)):
            r0 = jnp.clip(r - NA_WIN_R // 2, 0, grid_rows - NA_WIN_R)
            dr0 = r0 - r + NA_WIN_R - 1
            rows = pl.ds(pl.multiple_of(nctx + r * GRID_W, GRID_W), GRID_W)
            keys = pl.ds(pl.multiple_of(nctx + r0 * GRID_W, GRID_W), NA_WIN_R * GRID_W)
            bias = jnp.concatenate([t2_ref[0, dr0 + 2 * m] for m in range(NA_PAIRS)], axis=1)
            blocks.append((rows, [(kn_ref[keys, :], vb_ref[keys, :], bias), (kn_ref[0:nctx, :], vb_ref[0:nctx, :], None)]))
        attend(blocks)
        return carry

    lax.fori_loop(0, grid_rows // NA_UNROLL, grid_rows_step, 0)


def _na_attention(p, q_gain, k_gain, rpb, nctx):
    B, S, D3 = p.shape
    D = D3 // 3
    grid_rows = (S - nctx) // GRID_W
    assert (S - nctx) % GRID_W == 0 and grid_rows >= NA_WIN_R and S % NA_ROWS == 0 and nctx % GRID_W == 0
    ng = D // HEAD_DIM
    assert grid_rows % NA_UNROLL == 0
    t2 = _na_bias_table(rpb).reshape(ng, NA_GROUP, 2 * NA_WIN_R - 2, GRID_W, 2 * GRID_W)
    t2 = jnp.swapaxes(t2, 1, 2).reshape(ng, 2 * NA_WIN_R - 2, NA_GROUP * GRID_W, 2 * GRID_W)
    lane_head = np.arange(HEAD_DIM) // NA_HEAD_DIM
    gm = jnp.asarray((lane_head[:, None] == lane_head[None, :]) / NA_HEAD_DIM, BF16)
    hm = jnp.asarray((np.arange(NA_GROUP)[:, None, None] == lane_head[None, None, :]), F32)
    tile = lambda g: jnp.tile(g.astype(F32), NA_GROUP).reshape(1, HEAD_DIM)
    est = (2 * (3 * S * 128 * 4 + t2.shape[1] * t2.shape[2] * 128 * 4 + S * 128 * 2) + 3 * S * 128 * 2
           + 6 * NA_UNROLL * NA_GROUP * GRID_W * (NA_WIN_R * GRID_W + nctx) * 4)
    col = lambda k: pl.BlockSpec((1, S, HEAD_DIM), lambda b, g, k=k: (b, 0, k * ng + g))
    return pl.pallas_call(
        functools.partial(_na_kernel, S=S, nctx=nctx, grid_rows=grid_rows),
        grid=(B, ng),
        in_specs=[
            col(0), col(1), col(2),
            pl.BlockSpec((1, HEAD_DIM), lambda b, g: (0, 0)),
            pl.BlockSpec((1, HEAD_DIM), lambda b, g: (0, 0)),
            pl.BlockSpec(gm.shape, lambda b, g: (0, 0)),
            pl.BlockSpec(hm.shape, lambda b, g: (0, 0, 0)),
            pl.BlockSpec((1,) + t2.shape[1:], lambda b, g: (g, 0, 0, 0)),
        ],
        out_specs=pl.BlockSpec((1, S, HEAD_DIM), lambda b, g: (b, 0, g)),
        out_shape=jax.ShapeDtypeStruct((B, S, D), BF16),
        scratch_shapes=[pltpu.VMEM((S, HEAD_DIM), BF16)] * 3,
        compiler_params=_params(("arbitrary", "arbitrary"), est),
        name="na_attention",
    )(p, p, p, tile(q_gain), tile(k_gain), gm, hm, t2)


def _hgrn2_lower_bound(lb_logits, j):
    cs = jnp.cumsum(jax.nn.softmax(lb_logits.astype(F32), axis=0), axis=0)
    return cs[j] - cs[0]


def kernel(x, c, ctx, c_ctx, ada_w, ada_b, norm_mix_g, norm_ffn_g, hg_w_in, hg_lb_logits, hg_norm_g, hg_w_out,
           gdn_w_in, gdn_conv_w, gdn_a_log, gdn_dt_bias, gdn_norm_g, gdn_w_out, na_w_qkv, na_q_norm_g, na_k_norm_g,
           na_rpb, na_w_out, ffn_w_up, ffn_conv_w, ffn_w_down):
    depth = ada_w.shape[0]
    nctx = ctx.shape[1]
    D = x.shape[2]
    xs = jnp.concatenate([ctx, x], axis=1)
    mods = _ada_params(c, c_ctx, ada_w, ada_b)
    for i in range(depth):
        m, j = i % 3, i // 3
        mod = mods[i]
        if m == 0:
            p = _mod_linear(xs, mod, norm_mix_g[i], hg_w_in[j].astype(BF16), nctx, 0, 1, F32)
            y = _hgrn2_scan(p, _hgrn2_lower_bound(hg_lb_logits, j), hg_norm_g[j], nctx)
            w_out = hg_w_out[j]
        elif m == 1:
            w_in = gdn_w_in[j].astype(BF16)
            p = _mod_linear(xs, mod, norm_mix_g[i], w_in[:, :6 * D], nctx, 0, 1, F32)
            ba = _mod_linear(xs, mod, norm_mix_g[i], w_in[:, 6 * D:], nctx, 0, 1, F32)
            y = _gdn_scan(p, ba, gdn_conv_w[j], gdn_a_log[j], gdn_dt_bias[j], gdn_norm_g[j], nctx)
            w_out = gdn_w_out[j]
        else:
            p = _mod_linear(xs, mod, norm_mix_g[i], na_w_qkv[j].astype(BF16), nctx, 0, 1, F32)
            y = _na_attention(p, na_q_norm_g[j], na_k_norm_g[j], na_rpb[j], nctx)
            w_out = na_w_out[j]
        xs = _out_linear(y, w_out.astype(BF16), xs, mod, nctx, 2)
        w_up, conv_w, w_down = _ffn_weights(ffn_w_up[i], ffn_conv_w[i], ffn_w_down[i])
        xs = _conv_ffn(xs, mod, norm_ffn_g[i], w_up, conv_w, w_down, nctx)
    return xs[:, nctx:]
```

```python
import functools

import numpy as np
import jax
import jax.numpy as jnp
from jax import lax
from jax.experimental import pallas as pl
from jax.experimental.pallas import tpu as pltpu

F32 = jnp.float32
BF16 = jnp.bfloat16

RMS_EPS = 1e-6
N_MOD = 6
HEAD_DIM = 128
SCAN_CHUNK = 64
GRID_W = 64
NA_HEAD_DIM = 32
NA_WIN_R = 8
NA_WIN_C = 16
NA_GROUP = HEAD_DIM // NA_HEAD_DIM
GDN_CONV = 5
FFN_CONV = 3
FFN_TILE = 512
HALO = 16
NEG_BIG = -1e30

V7X_VMEM_BYTES = 64 * 1024 * 1024
VMEM_LIMIT_CAP = 56 * 1024 * 1024

NT_DIMS = (((1,), (1,)), ((), ()))
TN_DIMS = (((0,), (0,)), ((), ()))


def _params(sem, est_bytes, **kw):
    limit = int(min(VMEM_LIMIT_CAP, max(32 * 1024 * 1024, est_bytes * 5 // 4)))
    return pltpu.CompilerParams(dimension_semantics=sem, vmem_limit_bytes=limit, **kw)


def _pick_tile(n, cap, mult):
    best = None
    for t in range(mult, min(n, cap) + 1, mult):
        if n % t == 0:
            best = t
    assert best is not None, (n, cap, mult)
    return best


def _silu(x):
    return x * jax.nn.sigmoid(x)


def _dot(a, b):
    return jnp.dot(a, b, preferred_element_type=F32)


def _dot_nt(a, b):
    return lax.dot_general(a, b, NT_DIMS, preferred_element_type=F32)


def _dot_tn(a, b):
    return lax.dot_general(a, b, TN_DIMS, preferred_element_type=F32)


def _split_bf16(x, n):
    parts = []
    for _ in range(n):
        p = x.astype(BF16)
        parts.append(p)
        x = x - p.astype(F32)
    return parts


def _segment_rows(mod_ref, idx, is_ctx):
    return jnp.where(is_ctx, mod_ref[0, 0, idx:idx + 1, :], mod_ref[0, 1, idx:idx + 1, :])


def _modulate(x, row0, nctx, gain, mod_ref, i_shift, i_scale):
    ms = jnp.mean(x * x, axis=-1, keepdims=True)
    xn = x * lax.rsqrt(ms + RMS_EPS)
    is_ctx = row0 < nctx
    scale = _segment_rows(mod_ref, i_scale, is_ctx)
    shift = _segment_rows(mod_ref, i_shift, is_ctx)
    return xn * (gain * (1.0 + scale)) + shift


MOD_ROWS = 32


def _modulate_tile(h_ref, dst0, x_ref, n, row0, nctx, gain, mod_ref, i_shift, i_scale):
    def body(s, carry):
        r = pl.multiple_of(s * MOD_ROWS, MOD_ROWS)
        h = _modulate(x_ref[0, pl.ds(r, MOD_ROWS), :], row0 + r, nctx, gain, mod_ref, i_shift, i_scale)
        h_ref[pl.ds(dst0 + r, MOD_ROWS), :] = h.astype(BF16)
        return carry

    lax.fori_loop(0, n // MOD_ROWS, body, 0, unroll=4)


def _ada_kernel(c_ref, w_ref, b_ref, o_ref):
    s = _silu(c_ref[...]).astype(BF16)
    o_ref[0] = _dot(s, w_ref[0]) + b_ref[0]


def _ada_params(c, c_ctx, ada_w, ada_b):
    L, D, N = ada_w.shape
    B = c.shape[0]
    rows = -(-(B + 1) // 16) * 16
    cc = jnp.zeros((rows, D), F32).at[:B].set(c).at[B].set(c_ctx)
    tn = _pick_tile(N, 1024, 128)
    est = 2 * (rows * D * 4 + D * tn * 2 + tn * 4 + rows * tn * 4)
    out = pl.pallas_call(
        _ada_kernel,
        grid=(L, N // tn),
        in_specs=[
            pl.BlockSpec((rows, D), lambda l, j: (0, 0)),
            pl.BlockSpec((1, D, tn), lambda l, j: (l, 0, j)),
            pl.BlockSpec((1, 1, tn), lambda l, j: (l, 0, j)),
        ],
        out_specs=pl.BlockSpec((1, rows, tn), lambda l, j: (l, 0, j)),
        out_shape=jax.ShapeDtypeStruct((L, rows, N), F32),
        compiler_params=_params(("arbitrary", "arbitrary"), est),
        name="ada_ln",
    )(cc, ada_w.astype(BF16), ada_b.reshape(L, 1, N))
    lat = out[:, :B].reshape(L, B, 1, N_MOD, D)
    ctx = jnp.broadcast_to(out[:, B].reshape(L, 1, 1, N_MOD, D), (L, B, 1, N_MOD, D))
    return jnp.concatenate([ctx, lat], axis=2)


def _mod_linear_kernel(x_ref, mod_ref, g_ref, w_ref, o_ref, h_ref, *, tm, nctx, i_shift, i_scale):
    @pl.when(pl.program_id(2) == 0)
    def _():
        _modulate_tile(h_ref, 0, x_ref, tm, pl.program_id(1) * tm, nctx, g_ref[...], mod_ref, i_shift, i_scale)

    o_ref[0] = _dot(h_ref[...], w_ref[...]).astype(o_ref.dtype)


def _mod_linear(xs, mod, gain, w, nctx, i_shift, i_scale, out_dtype):
    B, S, D = xs.shape
    N = w.shape[1]
    tm = _pick_tile(S, 768, 16)
    tn = N if N < 128 else _pick_tile(N, 1024, 128)
    assert tm % MOD_ROWS == 0 and nctx % MOD_ROWS == 0
    osz = jnp.dtype(out_dtype).itemsize
    est = 2 * (tm * D * 4 + 2 * N_MOD * D * 4 + D * tn * 2 + tm * tn * osz) + tm * D * 2 + 3 * tm * D * 4
    return pl.pallas_call(
        functools.partial(_mod_linear_kernel, tm=tm, nctx=nctx, i_shift=i_shift, i_scale=i_scale),
        grid=(B, S // tm, N // tn),
        in_specs=[
            pl.BlockSpec((1, tm, D), lambda b, i, j: (b, i, 0)),
            pl.BlockSpec((1, 2, N_MOD, D), lambda b, i, j: (b, 0, 0, 0)),
            pl.BlockSpec((1, D), lambda b, i, j: (0, 0)),
            pl.BlockSpec((D, tn), lambda b, i, j: (0, j)),
        ],
        out_specs=pl.BlockSpec((1, tm, tn), lambda b, i, j: (b, i, j)),
        out_shape=jax.ShapeDtypeStruct((B, S, N), out_dtype),
        scratch_shapes=[pltpu.VMEM((tm, D), BF16)],
        compiler_params=_params(("arbitrary", "arbitrary", "arbitrary"), est),
        name="mod_linear",
    )(xs, mod, gain.reshape(1, D), w)


def _out_linear_kernel(y_ref, w_ref, x_ref, mod_ref, o_ref, *, tm, nctx, i_gate):
    t = pl.program_id(1) * tm + lax.broadcasted_iota(jnp.int32, (tm, 1), 0)
    gate = _segment_rows(mod_ref, i_gate, t < nctx)
    o_ref[0] = x_ref[0] + gate * _dot(y_ref[0], w_ref[...])


def _out_linear(y, w, xs, mod, nctx, i_gate):
    B, S, K = y.shape
    D = w.shape[1]
    tm = _pick_tile(S, 768, 16)
    tn = _pick_tile(D, 1024 if K <= 2048 else 512, 128)
    est = 2 * (tm * K * 2 + K * tn * 2 + 2 * tm * tn * 4 + 2 * N_MOD * tn * 4) + 2 * tm * tn * 4
    return pl.pallas_call(
        functools.partial(_out_linear_kernel, tm=tm, nctx=nctx, i_gate=i_gate),
        grid=(B, S // tm, D // tn),
        in_specs=[
            pl.BlockSpec((1, tm, K), lambda b, i, j: (b, i, 0)),
            pl.BlockSpec((K, tn), lambda b, i, j: (0, j)),
            pl.BlockSpec((1, tm, tn), lambda b, i, j: (b, i, j)),
            pl.BlockSpec((1, 2, N_MOD, tn), lambda b, i, j: (b, 0, 0, j)),
        ],
        out_specs=pl.BlockSpec((1, tm, tn), lambda b, i, j: (b, i, j)),
        out_shape=jax.ShapeDtypeStruct((B, S, D), F32),
        compiler_params=_params(("arbitrary", "arbitrary", "arbitrary"), est),
        name="out_linear",
    )(y, w, xs, mod)


def _ffn_up_kernel(xp_ref, x_ref, xn_ref, mod_ref, g_ref, wa_ref, wb_ref, ca_ref, cb_ref, o_ref, h_ref, *, tm, nctx, S):
    i, f = pl.program_id(1), pl.program_id(2)
    row0 = i * tm
    gain = g_ref[...]

    @pl.when(f == 0)
    def _():
        h_ref[0:HALO] = _modulate(xp_ref[0], row0 - HALO, nctx, gain, mod_ref, 3, 4).astype(BF16)
        _modulate_tile(h_ref, HALO, x_ref, tm, row0, nctx, gain, mod_ref, 3, 4)
        h_ref[HALO + tm:] = _modulate(xn_ref[0], row0 + tm, nctx, gain, mod_ref, 3, 4).astype(BF16)

    t = row0 + lax.broadcasted_iota(jnp.int32, (tm, 1), 0)
    has_prev = jnp.logical_and(t != 0, t != nctx).astype(F32)
    has_next = jnp.logical_and(t != nctx - 1, t != S - 1).astype(F32)
    rows = tm + 2 * HALO

    def conv(w_ref, c_ref):
        u = _dot(h_ref[...], w_ref[...])
        prev = pltpu.roll(u, 1, 0)[HALO:HALO + tm] * has_prev
        nxt = pltpu.roll(u, rows - 1, 0)[HALO:HALO + tm] * has_next
        return c_ref[0:1, :] * prev + c_ref[1:2, :] * u[HALO:HALO + tm] + c_ref[2:3, :] * nxt

    o_ref[0] = (_silu(conv(wa_ref, ca_ref)) * conv(wb_ref, cb_ref)).astype(BF16)


def _conv_ffn(xs, mod, gain, w_up, conv_w, w_down, nctx):
    B, S, D = xs.shape
    ffp = w_down.shape[0]
    tf = FFN_TILE
    nf = ffp // tf
    tm = _pick_tile(S, 768, HALO)
    assert tm % MOD_ROWS == 0 and nctx % MOD_ROWS == 0 and MOD_ROWS % HALO == 0
    nh = tm // HALO
    last = S // HALO - 1
    est = (2 * (tm * D * 4 + 2 * HALO * D * 4 + 2 * N_MOD * D * 4 + 2 * D * tf * 2 + 2 * 8 * tf * 4 + tm * tf * 2)
           + (tm + 2 * HALO) * D * 2 + 10 * (tm + 2 * HALO) * tf * 4)
    hidden = pl.pallas_call(
        functools.partial(_ffn_up_kernel, tm=tm, nctx=nctx, S=S),
        grid=(B, S // tm, nf),
        in_specs=[
            pl.BlockSpec((1, HALO, D), lambda b, i, f: (b, jnp.maximum(i * nh - 1, 0), 0)),
            pl.BlockSpec((1, tm, D), lambda b, i, f: (b, i, 0)),
            pl.BlockSpec((1, HALO, D), lambda b, i, f: (b, jnp.minimum((i + 1) * nh, last), 0)),
            pl.BlockSpec((1, 2, N_MOD, D), lambda b, i, f: (b, 0, 0, 0)),
            pl.BlockSpec((1, D), lambda b, i, f: (0, 0)),
            pl.BlockSpec((D, tf), lambda b, i, f: (0, f)),
            pl.BlockSpec((D, tf), lambda b, i, f: (0, nf + f)),
            pl.BlockSpec((FFN_CONV, tf), lambda b, i, f: (0, f)),
            pl.BlockSpec((FFN_CONV, tf), lambda b, i, f: (0, nf + f)),
        ],
        out_specs=pl.BlockSpec((1, tm, tf), lambda b, i, f: (b, i, f)),
        out_shape=jax.ShapeDtypeStruct((B, S, ffp), BF16),
        scratch_shapes=[pltpu.VMEM((tm + 2 * HALO, D), BF16)],
        compiler_params=_params(("arbitrary", "arbitrary", "arbitrary"), est),
        name="ffn_up",
    )(xs, xs, xs, mod, gain.reshape(1, D), w_up, w_up, conv_w, conv_w)
    return _out_linear(hidden, w_down, xs, mod, nctx, 5)


def _ffn_weights(w_up, conv_w, w_down):
    ff = w_down.shape[0]
    ffp = -(-ff // FFN_TILE) * FFN_TILE
    pad = ffp - ff

    def halves(a):
        return jnp.concatenate([jnp.pad(a[:, :ff], ((0, 0), (0, pad))), jnp.pad(a[:, ff:], ((0, 0), (0, pad)))], axis=1)

    return halves(w_up).astype(BF16), halves(conv_w), jnp.pad(w_down, ((0, pad), (0, 0))).astype(BF16)


def _level_tables(C, reverse):
    n_lev = int(np.log2(C))
    t = np.arange(C)
    mats, masks = [], []
    for lev in range(n_lev):
        half = 1 << lev
        parent = t // (2 * half)
        mid = parent * 2 * half + half
        second = (t % (2 * half)) >= half
        a = np.zeros((C, C), np.float32)
        for tt in range(C):
            if second[tt]:
                a[tt, mid[tt]:tt + 1] = 1.0
            else:
                a[tt, tt + 1:mid[tt]] = 1.0
        mats.append(a)
        masks.append(((parent[:, None] == parent[None, :]) & second[:, None] & ~second[None, :]).astype(np.float32))
    mats.append(np.tril(np.ones((C, C), np.float32)))
    mats.append(np.ones((C, C), np.float32))
    mats, masks = np.stack(mats), np.stack(masks)
    if reverse:
        mats, masks = mats[:, ::-1, ::-1], masks[:, ::-1, ::-1]
    return mats.reshape(-1, C), masks


def _chunk_index(i, d, nch, ncc):
    if d == 0:
        return i
    return jnp.where(i < ncc, ncc - 1 - i, nch - 1 - i + ncc)


READOUT_ROWS = 256
HG_GROUP = 4
SAFE_LOG_DECAY = -80.0


def _scan_readout(o_acc, gate_ref, ng_ref, y_ref, n_heads):
    def body(i, carry):
        rows = pl.ds(pl.multiple_of(i * READOUT_ROWS, READOUT_ROWS), READOUT_ROWS)
        for h in range(n_heads):
            sl = slice(h * HEAD_DIM, (h + 1) * HEAD_DIM)
            o = o_acc[rows, sl]
            y = o * lax.rsqrt(jnp.mean(o * o, axis=-1, keepdims=True) + RMS_EPS) * ng_ref[...]
            y_ref[0, rows, sl] = (y * _silu(gate_ref[0, rows, sl])).astype(BF16)
        return carry

    lax.fori_loop(0, o_acc.shape[0] // READOUT_ROWS, body, 0)


def _hgrn2_kernel(q_ref, v_ref, f0_ref, f1_ref, gt_ref, lb_ref, ng_ref, cm_ref, mk_ref, y_ref, o_acc, st_ref,
                  *, S, nctx, hb):
    C = SCAN_CHUNK
    nch, ncc = S // C, nctx // C
    n_lev = mk_ref.shape[1] - 1
    f_refs = (f0_ref, f1_ref)
    st_ref[...] = jnp.zeros(st_ref.shape, F32)

    def zero(i, carry):
        o_acc[pl.ds(pl.multiple_of(i * C, C), C), :] = jnp.zeros((C, o_acc.shape[1]), F32)
        return carry

    lax.fori_loop(0, nch, zero, 0)

    def chunk(i, carry):
        steps = [(j, d) for j in range(HG_GROUP) for d in range(2)]
        rows = [pl.ds(pl.multiple_of(_chunk_index(i * HG_GROUP + j, d, nch, ncc) * C, C), C) for j, d in steps]
        q, k, v, logf = [], [], [], []
        for (j, d), rw in zip(steps, rows):
            pf = f_refs[d][0, rw, :]
            e = jnp.exp(-jnp.abs(pf))
            r = 1.0 / (1.0 + e)
            a = lb_ref[d, 0:1, :]
            b = lb_ref[d, 1:2, :] + (jnp.minimum(pf, 0.0) + jnp.log(r))
            logf.append(jnp.maximum(a, b) + jnp.log(1.0 + jnp.exp(-jnp.abs(a - b))))
            k.append(lb_ref[d, 2:3, :] * jnp.where(pf >= 0.0, e * r, r))
            q.append(_silu(q_ref[0, rw, :]))
            v.append(v_ref[0, rw, :])
        inst = [(n, d, h, slice(h * HEAD_DIM, (h + 1) * HEAD_DIM)) for n, (_, d) in enumerate(steps) for h in range(hb)]
        qh = [q[n][:, sl] for n, _, _, sl in inst]
        kh = [k[n][:, sl] for n, _, _, sl in inst]
        vh = [v[n][:, sl] for n, _, _, sl in inst]
        vb = [x.astype(BF16) for x in vh]

        def run(single_level):
            first = n_lev * C if single_level else 0
            seg = [_dot(cm_ref[d, first:, :], jnp.concatenate(_split_bf16(lf, 2), axis=0)) for lf, (_, d) in zip(logf, steps)]
            sg = [seg[n][:, sl] for n, _, _, sl in inst]
            cum = [x[-2 * C:-C] for x in sg]
            tot = [x[-C:] for x in sg]
            qe = [(a * jnp.exp(c)).astype(BF16) for a, c in zip(qh, cum)]
            if single_level:
                att = [mk_ref[d, n_lev] * _dot_nt(a, (b * jnp.exp(-c)).astype(BF16))
                       for (_, d, _, _), a, b, c in zip(inst, qe, kh, cum)]
            else:
                pair = []
                for a, b, x in zip(qh, kh, sg):
                    a, b = a.astype(BF16), b.astype(BF16)
                    e = [jnp.exp(x[lev * C:(lev + 1) * C]).astype(BF16) for lev in range(n_lev)]
                    pair.append([_dot_nt(a * ee, b * ee) for ee in e])
                att = [functools.reduce(jnp.add, [mk_ref[d, lev] * p[lev] for lev in range(n_lev)])
                       for (_, d, _, _), p in zip(inst, pair)]
            upd = [_dot_tn(x, (b * jnp.exp(t - c)).astype(BF16)) for x, b, t, c in zip(vb, kh, tot, cum)]
            st, cur = [], {}
            for m, (n, d, h, _) in enumerate(inst):
                s = cur[d, h] if (d, h) in cur else st_ref[d, h]
                st.append(s)
                cur[d, h] = s * jnp.exp(tot[m][0:1, :]) + upd[m]
            o_inter = [_dot_nt(a, s.astype(BF16)) for a, s in zip(qe, st)]
            o_intra = [_dot(a.astype(BF16), x) for a, x in zip(att, vb)]
            for m, (n, d, h, sl) in enumerate(inst):
                o_acc[rows[n], sl] += o_intra[m] + o_inter[m] + jnp.sum(qh[m] * kh[m], axis=-1, keepdims=True) * vh[m]
            for (d, h), s in cur.items():
                st_ref[d, h] = s

        safe = functools.reduce(jnp.minimum, [jnp.min(jnp.sum(lf, axis=0, keepdims=True)) for lf in logf]) > SAFE_LOG_DECAY
        pl.when(safe)(functools.partial(run, True))
        pl.when(jnp.logical_not(safe))(functools.partial(run, False))
        return carry

    lax.fori_loop(0, nch // HG_GROUP, chunk, 0)

    _scan_readout(o_acc, gt_ref, ng_ref, y_ref, hb)


def _hgrn2_scan(p, lb, norm_g, nctx):
    B, S, W5 = p.shape
    W = W5 // 5
    hb = 2 if W % (2 * HEAD_DIM) == 0 else 1
    wb = hb * HEAD_DIM
    nhg = W // wb
    C = SCAN_CHUNK
    tabs = [_level_tables(C, rev) for rev in (False, True)]
    cm = jnp.asarray(np.stack([np.concatenate([t[0], t[0]], axis=1) for t in tabs]), BF16)
    mk = jnp.asarray(np.stack([np.concatenate([t[1], t[1].sum(0, keepdims=True)]) for t in tabs]), F32)
    lbp = jnp.stack([jnp.log(lb), jnp.log1p(-lb), 1.0 - lb], axis=1)
    est = 2 * (5 * S * wb * 4 + S * wb * 2) + S * wb * 4 + 16 * C * wb * 4 * 8
    col = lambda k: pl.BlockSpec((1, S, wb), lambda b, g, k=k: (b, 0, k * nhg + g))
    return pl.pallas_call(
        functools.partial(_hgrn2_kernel, S=S, nctx=nctx, hb=hb),
        grid=(B, nhg),
        in_specs=[
            col(0), col(1), col(2), col(3), col(4),
            pl.BlockSpec((2, 3, wb), lambda b, g: (0, 0, g)),
            pl.BlockSpec((1, HEAD_DIM), lambda b, g: (0, 0)),
            pl.BlockSpec(cm.shape, lambda b, g: (0, 0, 0)),
            pl.BlockSpec(mk.shape, lambda b, g: (0, 0, 0, 0)),
        ],
        out_specs=pl.BlockSpec((1, S, wb), lambda b, g: (b, 0, g)),
        out_shape=jax.ShapeDtypeStruct((B, S, W), BF16),
        scratch_shapes=[pltpu.VMEM((S, wb), F32), pltpu.VMEM((2, hb, HEAD_DIM, HEAD_DIM), F32)],
        compiler_params=_params(("arbitrary", "arbitrary"), est),
        name="hgrn2_scan",
    )(p, p, p, p, p, lbp, norm_g.reshape(1, HEAD_DIM), cm, mk)


GDN_ROWS = 256


def _softplus(x):
    return jnp.maximum(x, 0.0) + jnp.log1p(jnp.exp(-jnp.abs(x)))


def _conv5_silu(x_ref, w_ref, r0, S, nctx):
    n = GDN_ROWS + 16
    lo = pl.multiple_of(jnp.maximum(r0 - 8, 0), 8)
    hi = pl.multiple_of(jnp.minimum(r0 + GDN_ROWS, S - 8), 8)
    keep_prev = jnp.where(jnp.logical_or(r0 == 0, r0 == nctx), 0.0, 1.0)
    keep_next = jnp.where(jnp.logical_or(r0 + GDN_ROWS == nctx, r0 + GDN_ROWS == S), 0.0, 1.0)
    ext = jnp.concatenate([x_ref[0, pl.ds(lo, 8), :] * keep_prev, x_ref[0, pl.ds(r0, GDN_ROWS), :],
                           x_ref[0, pl.ds(hi, 8), :] * keep_next], axis=0)
    half = GDN_CONV // 2
    acc = w_ref[half:half + 1, :] * ext[8:8 + GDN_ROWS]
    for off in range(-half, half + 1):
        if off != 0:
            acc = acc + w_ref[off + half:off + half + 1, :] * pltpu.roll(ext, (-off) % n, 0)[8:8 + GDN_ROWS]
    return _silu(acc)


def _l2norm(x):
    return x * lax.rsqrt(jnp.sum(x * x, axis=-1, keepdims=True) + RMS_EPS)


GDN_GROUP = 4


def _gdn_kernel(pq_ref, pk_ref, pv_ref, pz_ref, bag_ref, bat_ref, cq_ref, ck_ref, cv_ref, gl_ref, gc_ref, ng_ref,
                cmc_ref, cmr_ref, mk_ref, y_ref, qn_ref, kn_ref, vv_ref, o_acc, st_ref, lhs_s, ku_s, au_s, dc_s, *, S, nctx):
    C, G = SCAN_CHUNK, GDN_GROUP
    nch, ncc = S // C, nctx // C
    ngrp, ncg = nch // G, ncc // G
    n_lev = mk_ref.shape[1] - 2

    def prologue(i, carry):
        r0 = pl.multiple_of(i * GDN_ROWS, GDN_ROWS)
        rows = pl.ds(r0, GDN_ROWS)
        qn_ref[rows, :] = _l2norm(_conv5_silu(pq_ref, cq_ref, r0, S, nctx)) * (HEAD_DIM ** -0.5)
        kn_ref[rows, :] = _l2norm(_conv5_silu(pk_ref, ck_ref, r0, S, nctx))
        vv_ref[rows, :] = _conv5_silu(pv_ref, cv_ref, r0, S, nctx)
        o_acc[rows, :] = jnp.zeros((GDN_ROWS, 2 * HEAD_DIM), F32)
        return carry

    lax.fori_loop(0, S // GDN_ROWS, prologue, 0)
    st_ref[...] = jnp.zeros(st_ref.shape, F32)

    def group_terms(chunks, between):
        rows = [pl.ds(pl.multiple_of(c * C, C), C) for c, _ in chunks]
        ba = [bag_ref[0, 0, r, :] for r in rows]
        beta8 = [jax.nn.sigmoid(x) for x in ba]
        g8 = [gl_ref[0, 0:1, :] * _softplus(x + gl_ref[0, 1:2, :]) for x in ba]
        g8r = [gc_ref[0, :, 0:1] * _softplus(bat_ref[0, 0, c] + gc_ref[0, :, 1:2]) for c, _ in chunks]
        cum8 = [_dot(cmc_ref[d], jnp.concatenate(_split_bf16(x, 3), axis=0)) for x, (_, d) in zip(g8, chunks)]
        cum8r = [_dot(jnp.concatenate(_split_bf16(x, 3), axis=1), cmr_ref[d]) for x, (_, d) in zip(g8r, chunks)]
        kc = [kn_ref[r, :] for r in rows]
        qc = [qn_ref[r, :] for r in rows]
        kb = [x.astype(BF16) for x in kc]
        kk = [_dot_nt(x, x) for x in kb]
        qk = [_dot_nt(x.astype(BF16), y) for x, y in zip(qc, kb)]
        inst = [(i, h) for i in range(len(chunks)) for h in range(2)]
        beta, gcum, gtot, dec_i, lmat, tinv = [], [], [], [], [], []
        for i, h in inst:
            d = chunks[i][1]
            lb, la = 4 * d + h, 4 * d + 2 + h
            strict, incl = mk_ref[d, n_lev], mk_ref[d, n_lev + 1]
            beta.append(beta8[i][:, lb:lb + 1])
            gcum.append(cum8[i][:C, la:la + 1])
            gtot.append(cum8[i][C:, la:la + 1])
            diff = gcum[-1] - cum8r[i][la:la + 1, :]
            dec_i.append(jnp.exp(jnp.where(incl > 0.5, diff, NEG_BIG)))
            lmat.append(beta[-1] * kk[i] * jnp.exp(jnp.where(strict > 0.5, diff, NEG_BIG)))
            tinv.append((incl - strict) - lmat[-1] * mk_ref[d, 0])
        for lev in range(1, n_lev):
            tb = [t.astype(BF16) for t in tinv]
            x = [_dot(t, (lm * mk_ref[chunks[i][1], lev]).astype(BF16)).astype(BF16) for t, lm, (i, _) in zip(tb, lmat, inst)]
            tinv = [t - _dot(xi, ti) for t, xi, ti in zip(tinv, x, tb)]
            if lev <= len(between):
                between[lev - 1]()
        rhs = [jnp.concatenate([vv_ref[rows[i], h * HEAD_DIM:(h + 1) * HEAD_DIM] * b, kc[i] * (b * jnp.exp(g))], axis=1)
               for (i, h), b, g in zip(inst, beta, gcum)]
        sol = [_dot(t.astype(BF16), r.astype(BF16)).astype(BF16) for t, r in zip(tinv, rhs)]
        ku_kw = [_dot_tn((kc[i] * jnp.exp(gt - g)).astype(BF16), s) for (i, _), gt, g, s in zip(inst, gtot, gcum, sol)]
        au_aw = [_dot((qk[i] * dc).astype(BF16), s) for (i, _), dc, s in zip(inst, dec_i, sol)]
        for n, (i, h) in enumerate(inst):
            qt = qc[i] * jnp.exp(gcum[n]) - au_aw[n][:, HEAD_DIM:]
            lhs_s[2 * i + h] = jnp.concatenate([ku_kw[n][:, HEAD_DIM:], qt], axis=0).astype(BF16)
            ku_s[2 * i + h] = ku_kw[n][:, :HEAD_DIM]
            au_s[2 * i + h] = au_aw[n][:, :HEAD_DIM]
            dc_s[2 * i + h] = jnp.broadcast_to(jnp.exp(gtot[n][0:1, :]), (8, HEAD_DIM))

    chains = [(d, h) for d in range(2) for h in range(2)]

    def group_chunks(gi):
        base = [_chunk_index(gi, d, ngrp, ncg) * G for d in range(2)]
        return base, [(base[d] + j, d) for d in range(2) for j in range(G)]

    def state_step(gi, step):
        base, _ = group_chunks(gi)
        js = [step if d == 0 else G - 1 - step for d, _ in chains]
        slots = [(d * G + j) * 2 + h for j, (d, h) in zip(js, chains)]
        st = [st_ref[d, h] for d, h in chains]
        r = [_dot(lhs_s[n], s.astype(BF16)) for n, s in zip(slots, st)]
        for n, j, (d, h), s, rr in zip(slots, js, chains, st, r):
            rows = pl.ds(pl.multiple_of((base[d] + j) * C, C), C)
            o_acc[rows, h * HEAD_DIM:(h + 1) * HEAD_DIM] += rr[HEAD_DIM:] + au_s[n]
            st_ref[d, h] = dc_s[n, 0:1, :] * s - rr[:HEAD_DIM] + ku_s[n]

    def group(gi, carry):
        group_terms(group_chunks(gi)[1], [functools.partial(state_step, gi - 1, step) for step in range(G)])
        return carry

    group_terms(group_chunks(0)[1], [])
    lax.fori_loop(1, ngrp, group, 0)
    for step in range(G):
        state_step(ngrp - 1, step)

    _scan_readout(o_acc, pz_ref, ng_ref, y_ref, 2)


def _gdn_scan(p, ba, conv_w, a_log, dt_bias, norm_g, nctx):
    B, S, W6 = p.shape
    W = W6 // 6
    nq = W // HEAD_DIM
    C, G = SCAN_CHUNK, GDN_GROUP
    assert ba.shape[2] == 8 * nq and S % GDN_ROWS == 0 and nctx % GDN_ROWS == 0 and S % (C * G) == 0 and nctx % (C * G) == 0
    n_lev = int(np.log2(C))
    tri, masks = [], []
    for rev in (False, True):
        mats, mk = _level_tables(C, rev)
        tri.append(mats[n_lev * C:])
        strict = mk.sum(0)
        masks.append(np.concatenate([mk, strict[None], (strict + np.eye(C, dtype=np.float32))[None]], axis=0))
    cmc = jnp.asarray(np.stack([np.concatenate([t, t, t], axis=1) for t in tri]), BF16)
    cmr = jnp.asarray(np.stack([np.concatenate([t[:C].T, t[:C].T, t[:C].T], axis=0) for t in tri]), BF16)
    mk = jnp.asarray(np.stack(masks), F32)

    def per_group(a):
        a = a.reshape(a.shape[:-1] + (nq, 2))
        return jnp.moveaxis(a, -2, 0).reshape((nq,) + a.shape[:-4] + (8,))

    bag = jnp.moveaxis(per_group(ba.reshape(B, S, 2, 2, 2 * nq)), 0, 1)
    bat = jnp.swapaxes(bag.reshape(B, nq, S // C, C, 8), 3, 4)
    zeros = jnp.zeros((2, 2 * nq), F32)
    neg_rate = per_group(jnp.stack([zeros, -jnp.exp(a_log.astype(F32))], axis=1))
    dtb = per_group(jnp.stack([zeros, dt_bias.astype(F32)], axis=1))
    gl = jnp.stack([neg_rate, dtb], axis=1)
    q128, q256 = W // HEAD_DIM, W // (2 * HEAD_DIM)
    est = (2 * (2 * S * 128 * 4 + 2 * S * 256 * 4 + S * 128 * 4 + (S // C) * 8 * 128 * 4 + S * 256 * 2)
           + 2 * S * 128 * 4 + 2 * S * 256 * 4 + 16 * (3 * 128 * 128 * 4))
    return pl.pallas_call(
        functools.partial(_gdn_kernel, S=S, nctx=nctx),
        grid=(B, nq),
        in_specs=[
            pl.BlockSpec((1, S, HEAD_DIM), lambda b, g: (b, 0, g)),
            pl.BlockSpec((1, S, HEAD_DIM), lambda b, g: (b, 0, q128 + g)),
            pl.BlockSpec((1, S, 2 * HEAD_DIM), lambda b, g: (b, 0, 2 * q256 + g)),
            pl.BlockSpec((1, S, 2 * HEAD_DIM), lambda b, g: (b, 0, 4 * q256 + g)),
            pl.BlockSpec((1, 1, S, 8), lambda b, g: (b, g, 0, 0)),
            pl.BlockSpec((1, 1, S // C, 8, C), lambda b, g: (b, g, 0, 0, 0)),
            pl.BlockSpec((GDN_CONV, HEAD_DIM), lambda b, g: (0, g)),
            pl.BlockSpec((GDN_CONV, HEAD_DIM), lambda b, g: (0, q128 + g)),
            pl.BlockSpec((GDN_CONV, 2 * HEAD_DIM), lambda b, g: (0, 2 * q256 + g)),
            pl.BlockSpec((1, 2, 8), lambda b, g: (g, 0, 0)),
            pl.BlockSpec((1, 8, 2), lambda b, g: (g, 0, 0)),
            pl.BlockSpec((1, HEAD_DIM), lambda b, g: (0, 0)),
            pl.BlockSpec(cmc.shape, lambda b, g: (0, 0, 0)),
            pl.BlockSpec(cmr.shape, lambda b, g: (0, 0, 0)),
            pl.BlockSpec(mk.shape, lambda b, g: (0, 0, 0, 0)),
        ],
        out_specs=pl.BlockSpec((1, S, 2 * HEAD_DIM), lambda b, g: (b, 0, g)),
        out_shape=jax.ShapeDtypeStruct((B, S, 2 * W), BF16),
        scratch_shapes=[pltpu.VMEM((S, HEAD_DIM), F32), pltpu.VMEM((S, HEAD_DIM), F32), pltpu.VMEM((S, 2 * HEAD_DIM), F32),
                        pltpu.VMEM((S, 2 * HEAD_DIM), F32), pltpu.VMEM((2, 2, HEAD_DIM, HEAD_DIM), F32),
                        pltpu.VMEM((4 * G, HEAD_DIM + C, HEAD_DIM), BF16), pltpu.VMEM((4 * G, HEAD_DIM, HEAD_DIM), F32),
                        pltpu.VMEM((4 * G, C, HEAD_DIM), F32), pltpu.VMEM((4 * G, 8, HEAD_DIM), F32)],
        compiler_params=_params(("arbitrary", "arbitrary"), est),
        name="gdn_scan",
    )(p, p, p, p, bag, bat, conv_w, conv_w, conv_w, gl, jnp.swapaxes(gl, 1, 2), norm_g.reshape(1, HEAD_DIM), cmc, cmr, mk)


NA_ROWS = 256
NA_PAIRS = NA_WIN_R // 2
NA_UNROLL = 4


def _na_bias_table(rpb):
    q = np.arange(GRID_W)[:, None]
    kc = np.arange(GRID_W)[None, :]
    c0 = np.clip(q - NA_WIN_C // 2, 0, GRID_W - NA_WIN_C)
    valid = (kc >= c0) & (kc < c0 + NA_WIN_C)
    idx = np.clip(kc - q + NA_WIN_C - 1, 0, 2 * NA_WIN_C - 2)
    t = jnp.where(valid, rpb.astype(F32)[:, :, idx], NEG_BIG)
    return jnp.concatenate([t[:, :-1], t[:, 1:]], axis=-1)


def _softmax_pv(scores, values):
    m = functools.reduce(jnp.maximum, [jnp.max(s, axis=-1, keepdims=True) for s in scores])
    ps = [jnp.exp(s - m) for s in scores]
    den = functools.reduce(jnp.add, [jnp.sum(p, axis=-1, keepdims=True) for p in ps])
    num = functools.reduce(jnp.add, [_dot(p.astype(BF16), v) for p, v in zip(ps, values)])
    return num / den


def _na_kernel(q_ref, k_ref, v_ref, qg_ref, kg_ref, gm_ref, hm_ref, t2_ref, o_ref, qn_ref, kn_ref, vb_ref,
               *, S, nctx, grid_rows):
    scale = NA_HEAD_DIM ** -0.5

    def head_rms(x):
        hi, lo = _split_bf16(x * x, 2)
        ms = _dot(hi, gm_ref[...]) + _dot(lo, gm_ref[...])
        return x * lax.rsqrt(ms + RMS_EPS)

    def prologue(i, carry):
        rows = pl.ds(pl.multiple_of(i * NA_ROWS, NA_ROWS), NA_ROWS)
        qn_ref[rows, :] = (head_rms(q_ref[0, rows, :]) * (qg_ref[...] * scale)).astype(BF16)
        kn_ref[rows, :] = (head_rms(k_ref[0, rows, :]) * kg_ref[...]).astype(BF16)
        vb_ref[rows, :] = v_ref[0, rows, :].astype(BF16)
        return carry

    lax.fori_loop(0, S // NA_ROWS, prologue, 0)

    def attend(blocks):
        q4 = [jnp.concatenate([qn_ref[rows, :] * hm_ref[j].astype(BF16) for j in range(NA_GROUP)], axis=0)
              for rows, _ in blocks]
        scores = [[_dot_nt(q, k) if bias is None else _dot_nt(q, k) + bias for k, _, bias in parts]
                  for q, (_, parts) in zip(q4, blocks)]
        for (rows, parts), sc in zip(blocks, scores):
            o4 = _softmax_pv(sc, [v for _, v, _ in parts])
            out = functools.reduce(jnp.add, [o4[j * GRID_W:(j + 1) * GRID_W] * hm_ref[j] for j in range(NA_GROUP)])
            o_ref[0, rows, :] = out.astype(BF16)

    k_ctx, v_ctx = kn_ref[0:nctx, :], vb_ref[0:nctx, :]
    attend([(pl.ds(i * GRID_W, GRID_W), [(k_ctx, v_ctx, None)]) for i in range(nctx // GRID_W)])

    def grid_rows_step(i, carry):
        blocks = []
        for r in (NA_UNROLL * i + u for u in range(NA_UNROLL)):
            r0 = jnp.clip(r - NA_WIN_R // 2, 0, grid_rows - NA_WIN_R)
            dr0 = r0 - r + NA_WIN_R - 1
            rows = pl.ds(pl.multiple_of(nctx + r * GRID_W, GRID_W), GRID_W)
            keys = pl.ds(pl.multiple_of(nctx + r0 * GRID_W, GRID_W), NA_WIN_R * GRID_W)
            bias = jnp.concatenate([t2_ref[0, dr0 + 2 * m] for m in range(NA_PAIRS)], axis=1)
            blocks.append((rows, [(kn_ref[keys, :], vb_ref[keys, :], bias), (kn_ref[0:nctx, :], vb_ref[0:nctx, :], None)]))
        attend(blocks)
        return carry

    lax.fori_loop(0, grid_rows // NA_UNROLL, grid_rows_step, 0)


def _na_attention(p, q_gain, k_gain, rpb, nctx):
    B, S, D3 = p.shape
    D = D3 // 3
    grid_rows = (S - nctx) // GRID_W
    assert (S - nctx) % GRID_W == 0 and grid_rows >= NA_WIN_R and S % NA_ROWS == 0 and nctx % GRID_W == 0
    ng = D // HEAD_DIM
    assert grid_rows % NA_UNROLL == 0
    t2 = _na_bias_table(rpb).reshape(ng, NA_GROUP, 2 * NA_WIN_R - 2, GRID_W, 2 * GRID_W)
    t2 = jnp.swapaxes(t2, 1, 2).reshape(ng, 2 * NA_WIN_R - 2, NA_GROUP * GRID_W, 2 * GRID_W)
    lane_head = np.arange(HEAD_DIM) // NA_HEAD_DIM
    gm = jnp.asarray((lane_head[:, None] == lane_head[None, :]) / NA_HEAD_DIM, BF16)
    hm = jnp.asarray((np.arange(NA_GROUP)[:, None, None] == lane_head[None, None, :]), F32)
    tile = lambda g: jnp.tile(g.astype(F32), NA_GROUP).reshape(1, HEAD_DIM)
    est = (2 * (3 * S * 128 * 4 + t2.shape[1] * t2.shape[2] * 128 * 4 + S * 128 * 2) + 3 * S * 128 * 2
           + 6 * NA_UNROLL * NA_GROUP * GRID_W * (NA_WIN_R * GRID_W + nctx) * 4)
    col = lambda k: pl.BlockSpec((1, S, HEAD_DIM), lambda b, g, k=k: (b, 0, k * ng + g))
    return pl.pallas_call(
        functools.partial(_na_kernel, S=S, nctx=nctx, grid_rows=grid_rows),
        grid=(B, ng),
        in_specs=[
            col(0), col(1), col(2),
            pl.BlockSpec((1, HEAD_DIM), lambda b, g: (0, 0)),
            pl.BlockSpec((1, HEAD_DIM), lambda b, g: (0, 0)),
            pl.BlockSpec(gm.shape, lambda b, g: (0, 0)),
            pl.BlockSpec(hm.shape, lambda b, g: (0, 0, 0)),
            pl.BlockSpec((1,) + t2.shape[1:], lambda b, g: (g, 0, 0, 0)),
        ],
        out_specs=pl.BlockSpec((1, S, HEAD_DIM), lambda b, g: (b, 0, g)),
        out_shape=jax.ShapeDtypeStruct((B, S, D), BF16),
        scratch_shapes=[pltpu.VMEM((S, HEAD_DIM), BF16)] * 3,
        compiler_params=_params(("arbitrary", "arbitrary"), est),
        name="na_attention",
    )(p, p, p, tile(q_gain), tile(k_gain), gm, hm, t2)


def _hgrn2_lower_bound(lb_logits, j):
    cs = jnp.cumsum(jax.nn.softmax(lb_logits.astype(F32), axis=0), axis=0)
    return cs[j] - cs[0]


def kernel(x, c, ctx, c_ctx, ada_w, ada_b, norm_mix_g, norm_ffn_g, hg_w_in, hg_lb_logits, hg_norm_g, hg_w_out,
           gdn_w_in, gdn_conv_w, gdn_a_log, gdn_dt_bias, gdn_norm_g, gdn_w_out, na_w_qkv, na_q_norm_g, na_k_norm_g,
           na_rpb, na_w_out, ffn_w_up, ffn_conv_w, ffn_w_down):
    depth = ada_w.shape[0]
    nctx = ctx.shape[1]
    D = x.shape[2]
    xs = jnp.concatenate([ctx, x], axis=1)
    mods = _ada_params(c, c_ctx, ada_w, ada_b)
    for i in range(depth):
        m, j = i % 3, i // 3
        mod = mods[i]
        if m == 0:
            p = _mod_linear(xs, mod, norm_mix_g[i], hg_w_in[j].astype(BF16), nctx, 0, 1, F32)
            y = _hgrn2_scan(p, _hgrn2_lower_bound(hg_lb_logits, j), hg_norm_g[j], nctx)
            w_out = hg_w_out[j]
        elif m == 1:
            w_in = gdn_w_in[j].astype(BF16)
            p = _mod_linear(xs, mod, norm_mix_g[i], w_in[:, :6 * D], nctx, 0, 1, F32)
            ba = _mod_linear(xs, mod, norm_mix_g[i], w_in[:, 6 * D:], nctx, 0, 1, F32)
            y = _gdn_scan(p, ba, gdn_conv_w[j], gdn_a_log[j], gdn_dt_bias[j], gdn_norm_g[j], nctx)
            w_out = gdn_w_out[j]
        else:
            p = _mod_linear(xs, mod, norm_mix_g[i], na_w_qkv[j].astype(BF16), nctx, 0, 1, F32)
            y = _na_attention(p, na_q_norm_g[j], na_k_norm_g[j], na_rpb[j], nctx)
            w_out = na_w_out[j]
        xs = _out_linear(y, w_out.astype(BF16), xs, mod, nctx, 2)
        w_up, conv_w, w_down = _ffn_weights(ffn_w_up[i], ffn_conv_w[i], ffn_w_down[i])
        xs = _conv_ffn(xs, mod, norm_ffn_g[i], w_up, conv_w, w_down, nctx)
    return xs[:, nctx:]
```

```python
import functools

import numpy as np
import jax
import jax.numpy as jnp
from jax import lax
from jax.experimental import pallas as pl
from jax.experimental.pallas import tpu as pltpu

F32 = jnp.float32
BF16 = jnp.bfloat16

RMS_EPS = 1e-6
N_MOD = 6
HEAD_DIM = 128
SCAN_CHUNK = 64
GRID_W = 64
NA_HEAD_DIM = 32
NA_WIN_R = 8
NA_WIN_C = 16
NA_GROUP = HEAD_DIM // NA_HEAD_DIM
GDN_CONV = 5
FFN_CONV = 3
FFN_TILE = 512
HALO = 16
NEG_BIG = -1e30

V7X_VMEM_BYTES = 64 * 1024 * 1024
VMEM_LIMIT_CAP = 56 * 1024 * 1024

NT_DIMS = (((1,), (1,)), ((), ()))
TN_DIMS = (((0,), (0,)), ((), ()))


def _params(sem, est_bytes, **kw):
    limit = int(min(VMEM_LIMIT_CAP, max(32 * 1024 * 1024, est_bytes * 5 // 4)))
    return pltpu.CompilerParams(dimension_semantics=sem, vmem_limit_bytes=limit, **kw)


def _pick_tile(n, cap, mult):
    best = None
    for t in range(mult, min(n, cap) + 1, mult):
        if n % t == 0:
            best = t
    assert best is not None, (n, cap, mult)
    return best


def _silu(x):
    return x * jax.nn.sigmoid(x)


def _dot(a, b):
    return jnp.dot(a, b, preferred_element_type=F32)


def _dot_nt(a, b):
    return lax.dot_general(a, b, NT_DIMS, preferred_element_type=F32)


def _dot_tn(a, b):
    return lax.dot_general(a, b, TN_DIMS, preferred_element_type=F32)


def _split_bf16(x, n):
    parts = []
    for _ in range(n):
        p = x.astype(BF16)
        parts.append(p)
        x = x - p.astype(F32)
    return parts


def _segment_rows(mod_ref, idx, is_ctx):
    return jnp.where(is_ctx, mod_ref[0, 0, idx:idx + 1, :], mod_ref[0, 1, idx:idx + 1, :])


def _modulate(x, row0, nctx, gain, mod_ref, i_shift, i_scale):
    ms = jnp.mean(x * x, axis=-1, keepdims=True)
    xn = x * lax.rsqrt(ms + RMS_EPS)
    is_ctx = row0 < nctx
    scale = _segment_rows(mod_ref, i_scale, is_ctx)
    shift = _segment_rows(mod_ref, i_shift, is_ctx)
    return xn * (gain * (1.0 + scale)) + shift


MOD_ROWS = 32


def _modulate_tile(h_ref, dst0, x_ref, n, row0, nctx, gain, mod_ref, i_shift, i_scale):
    def body(s, carry):
        r = pl.multiple_of(s * MOD_ROWS, MOD_ROWS)
        h = _modulate(x_ref[0, pl.ds(r, MOD_ROWS), :], row0 + r, nctx, gain, mod_ref, i_shift, i_scale)
        h_ref[pl.ds(dst0 + r, MOD_ROWS), :] = h.astype(BF16)
        return carry

    lax.fori_loop(0, n // MOD_ROWS, body, 0, unroll=4)


def _ada_kernel(c_ref, w_ref, b_ref, o_ref):
    s = _silu(c_ref[...]).astype(BF16)
    o_ref[0] = _dot(s, w_ref[0]) + b_ref[0]


def _ada_params(c, c_ctx, ada_w, ada_b):
    L, D, N = ada_w.shape
    B = c.shape[0]
    rows = -(-(B + 1) // 16) * 16
    cc = jnp.zeros((rows, D), F32).at[:B].set(c).at[B].set(c_ctx)
    tn = _pick_tile(N, 1024, 128)
    est = 2 * (rows * D * 4 + D * tn * 2 + tn * 4 + rows * tn * 4)
    out = pl.pallas_call(
        _ada_kernel,
        grid=(L, N // tn),
        in_specs=[
            pl.BlockSpec((rows, D), lambda l, j: (0, 0)),
            pl.BlockSpec((1, D, tn), lambda l, j: (l, 0, j)),
            pl.BlockSpec((1, 1, tn), lambda l, j: (l, 0, j)),
        ],
        out_specs=pl.BlockSpec((1, rows, tn), lambda l, j: (l, 0, j)),
        out_shape=jax.ShapeDtypeStruct((L, rows, N), F32),
        compiler_params=_params(("arbitrary", "arbitrary"), est),
        name="ada_ln",
    )(cc, ada_w.astype(BF16), ada_b.reshape(L, 1, N))
    lat = out[:, :B].reshape(L, B, 1, N_MOD, D)
    ctx = jnp.broadcast_to(out[:, B].reshape(L, 1, 1, N_MOD, D), (L, B, 1, N_MOD, D))
    return jnp.concatenate([ctx, lat], axis=2)


def _mod_linear_kernel(x_ref, mod_ref, g_ref, w_ref, o_ref, h_ref, *, tm, nctx, i_shift, i_scale):
    @pl.when(pl.program_id(2) == 0)
    def _():
        _modulate_tile(h_ref, 0, x_ref, tm, pl.program_id(1) * tm, nctx, g_ref[...], mod_ref, i_shift, i_scale)

    o_ref[0] = _dot(h_ref[...], w_ref[...]).astype(o_ref.dtype)


def _mod_linear(xs, mod, gain, w, nctx, i_shift, i_scale, out_dtype):
    B, S, D = xs.shape
    N = w.shape[1]
    tm = _pick_tile(S, 768, 16)
    tn = N if N < 128 else _pick_tile(N, 1024, 128)
    assert tm % MOD_ROWS == 0 and nctx % MOD_ROWS == 0
    osz = jnp.dtype(out_dtype).itemsize
    est = 2 * (tm * D * 4 + 2 * N_MOD * D * 4 + D * tn * 2 + tm * tn * osz) + tm * D * 2 + 3 * tm * D * 4
    return pl.pallas_call(
        functools.partial(_mod_linear_kernel, tm=tm, nctx=nctx, i_shift=i_shift, i_scale=i_scale),
        grid=(B, S // tm, N // tn),
        in_specs=[
            pl.BlockSpec((1, tm, D), lambda b, i, j: (b, i, 0)),
            pl.BlockSpec((1, 2, N_MOD, D), lambda b, i, j: (b, 0, 0, 0)),
            pl.BlockSpec((1, D), lambda b, i, j: (0, 0)),
            pl.BlockSpec((D, tn), lambda b, i, j: (0, j)),
        ],
        out_specs=pl.BlockSpec((1, tm, tn), lambda b, i, j: (b, i, j)),
        out_shape=jax.ShapeDtypeStruct((B, S, N), out_dtype),
        scratch_shapes=[pltpu.VMEM((tm, D), BF16)],
        compiler_params=_params(("arbitrary", "arbitrary", "arbitrary"), est),
        name="mod_linear",
    )(xs, mod, gain.reshape(1, D), w)


def _out_linear_kernel(y_ref, w_ref, x_ref, mod_ref, o_ref, *, tm, nctx, i_gate):
    t = pl.program_id(1) * tm + lax.broadcasted_iota(jnp.int32, (tm, 1), 0)
    gate = _segment_rows(mod_ref, i_gate, t < nctx)
    o_ref[0] = x_ref[0] + gate * _dot(y_ref[0], w_ref[...])


def _out_linear(y, w, xs, mod, nctx, i_gate):
    B, S, K = y.shape
    D = w.shape[1]
    tm = _pick_tile(S, 768, 16)
    tn = _pick_tile(D, 1024 if K <= 2048 else 512, 128)
    est = 2 * (tm * K * 2 + K * tn * 2 + 2 * tm * tn * 4 + 2 * N_MOD * tn * 4) + 2 * tm * tn * 4
    return pl.pallas_call(
        functools.partial(_out_linear_kernel, tm=tm, nctx=nctx, i_gate=i_gate),
        grid=(B, S // tm, D // tn),
        in_specs=[
            pl.BlockSpec((1, tm, K), lambda b, i, j: (b, i, 0)),
            pl.BlockSpec((K, tn), lambda b, i, j: (0, j)),
            pl.BlockSpec((1, tm, tn), lambda b, i, j: (b, i, j)),
            pl.BlockSpec((1, 2, N_MOD, tn), lambda b, i, j: (b, 0, 0, j)),
        ],
        out_specs=pl.BlockSpec((1, tm, tn), lambda b, i, j: (b, i, j)),
        out_shape=jax.ShapeDtypeStruct((B, S, D), F32),
        compiler_params=_params(("arbitrary", "arbitrary", "arbitrary"), est),
        name="out_linear",
    )(y, w, xs, mod)


def _ffn_up_kernel(xp_ref, x_ref, xn_ref, mod_ref, g_ref, wa_ref, wb_ref, ca_ref, cb_ref, o_ref, h_ref, *, tm, nctx, S):
    i, f = pl.program_id(1), pl.program_id(2)
    row0 = i * tm
    gain = g_ref[...]

    @pl.when(f == 0)
    def _():
        h_ref[0:HALO] = _modulate(xp_ref[0], row0 - HALO, nctx, gain, mod_ref, 3, 4).astype(BF16)
        _modulate_tile(h_ref, HALO, x_ref, tm, row0, nctx, gain, mod_ref, 3, 4)
        h_ref[HALO + tm:] = _modulate(xn_ref[0], row0 + tm, nctx, gain, mod_ref, 3, 4).astype(BF16)

    t = row0 + lax.broadcasted_iota(jnp.int32, (tm, 1), 0)
    has_prev = jnp.logical_and(t != 0, t != nctx).astype(F32)
    has_next = jnp.logical_and(t != nctx - 1, t != S - 1).astype(F32)
    rows = tm + 2 * HALO

    def conv(w_ref, c_ref):
        u = _dot(h_ref[...], w_ref[...])
        prev = pltpu.roll(u, 1, 0)[HALO:HALO + tm] * has_prev
        nxt = pltpu.roll(u, rows - 1, 0)[HALO:HALO + tm] * has_next
        return c_ref[0:1, :] * prev + c_ref[1:2, :] * u[HALO:HALO + tm] + c_ref[2:3, :] * nxt

    o_ref[0] = (_silu(conv(wa_ref, ca_ref)) * conv(wb_ref, cb_ref)).astype(BF16)


def _conv_ffn(xs, mod, gain, w_up, conv_w, w_down, nctx):
    B, S, D = xs.shape
    ffp = w_down.shape[0]
    tf = FFN_TILE
    nf = ffp // tf
    tm = _pick_tile(S, 768, HALO)
    assert tm % MOD_ROWS == 0 and nctx % MOD_ROWS == 0 and MOD_ROWS % HALO == 0
    nh = tm // HALO
    last = S // HALO - 1
    est = (2 * (tm * D * 4 + 2 * HALO * D * 4 + 2 * N_MOD * D * 4 + 2 * D * tf * 2 + 2 * 8 * tf * 4 + tm * tf * 2)
           + (tm + 2 * HALO) * D * 2 + 10 * (tm + 2 * HALO) * tf * 4)
    hidden = pl.pallas_call(
        functools.partial(_ffn_up_kernel, tm=tm, nctx=nctx, S=S),
        grid=(B, S // tm, nf),
        in_specs=[
            pl.BlockSpec((1, HALO, D), lambda b, i, f: (b, jnp.maximum(i * nh - 1, 0), 0)),
            pl.BlockSpec((1, tm, D), lambda b, i, f: (b, i, 0)),
            pl.BlockSpec((1, HALO, D), lambda b, i, f: (b, jnp.minimum((i + 1) * nh, last), 0)),
            pl.BlockSpec((1, 2, N_MOD, D), lambda b, i, f: (b, 0, 0, 0)),
            pl.BlockSpec((1, D), lambda b, i, f: (0, 0)),
            pl.BlockSpec((D, tf), lambda b, i, f: (0, f)),
            pl.BlockSpec((D, tf), lambda b, i, f: (0, nf + f)),
            pl.BlockSpec((FFN_CONV, tf), lambda b, i, f: (0, f)),
            pl.BlockSpec((FFN_CONV, tf), lambda b, i, f: (0, nf + f)),
        ],
        out_specs=pl.BlockSpec((1, tm, tf), lambda b, i, f: (b, i, f)),
        out_shape=jax.ShapeDtypeStruct((B, S, ffp), BF16),
        scratch_shapes=[pltpu.VMEM((tm + 2 * HALO, D), BF16)],
        compiler_params=_params(("arbitrary", "arbitrary", "arbitrary"), est),
        name="ffn_up",
    )(xs, xs, xs, mod, gain.reshape(1, D), w_up, w_up, conv_w, conv_w)
    return _out_linear(hidden, w_down, xs, mod, nctx, 5)


def _ffn_weights(w_up, conv_w, w_down):
    ff = w_down.shape[0]
    ffp = -(-ff // FFN_TILE) * FFN_TILE
    pad = ffp - ff

    def halves(a):
        return jnp.concatenate([jnp.pad(a[:, :ff], ((0, 0), (0, pad))), jnp.pad(a[:, ff:], ((0, 0), (0, pad)))], axis=1)

    return halves(w_up).astype(BF16), halves(conv_w), jnp.pad(w_down, ((0, pad), (0, 0))).astype(BF16)


def _level_tables(C, reverse):
    n_lev = int(np.log2(C))
    t = np.arange(C)
    mats, masks = [], []
    for lev in range(n_lev):
        half = 1 << lev
        parent = t // (2 * half)
        mid = parent * 2 * half + half
        second = (t % (2 * half)) >= half
        a = np.zeros((C, C), np.float32)
        for tt in range(C):
            if second[tt]:
                a[tt, mid[tt]:tt + 1] = 1.0
            else:
                a[tt, tt + 1:mid[tt]] = 1.0
        mats.append(a)
        masks.append(((parent[:, None] == parent[None, :]) & second[:, None] & ~second[None, :]).astype(np.float32))
    mats.append(np.tril(np.ones((C, C), np.float32)))
    mats.append(np.ones((C, C), np.float32))
    mats, masks = np.stack(mats), np.stack(masks)
    if reverse:
        mats, masks = mats[:, ::-1, ::-1], masks[:, ::-1, ::-1]
    return mats.reshape(-1, C), masks


def _chunk_index(i, d, nch, ncc):
    if d == 0:
        return i
    return jnp.where(i < ncc, ncc - 1 - i, nch - 1 - i + ncc)


READOUT_ROWS = 256
HG_GROUP = 4
SAFE_LOG_DECAY = -80.0


def _scan_readout(o_acc, gate_ref, ng_ref, y_ref, n_heads):
    def body(i, carry):
        rows = pl.ds(pl.multiple_of(i * READOUT_ROWS, READOUT_ROWS), READOUT_ROWS)
        for h in range(n_heads):
            sl = slice(h * HEAD_DIM, (h + 1) * HEAD_DIM)
            o = o_acc[rows, sl]
            y = o * lax.rsqrt(jnp.mean(o * o, axis=-1, keepdims=True) + RMS_EPS) * ng_ref[...]
            y_ref[0, rows, sl] = (y * _silu(gate_ref[0, rows, sl])).astype(BF16)
        return carry

    lax.fori_loop(0, o_acc.shape[0] // READOUT_ROWS, body, 0)


def _hgrn2_kernel(q_ref, v_ref, f0_ref, f1_ref, gt_ref, lb_ref, ng_ref, cm_ref, mk_ref, y_ref, o_acc, st_ref,
                  *, S, nctx, hb):
    C = SCAN_CHUNK
    nch, ncc = S // C, nctx // C
    n_lev = mk_ref.shape[1] - 1
    f_refs = (f0_ref, f1_ref)
    st_ref[...] = jnp.zeros(st_ref.shape, F32)

    def zero(i, carry):
        o_acc[pl.ds(pl.multiple_of(i * C, C), C), :] = jnp.zeros((C, o_acc.shape[1]), F32)
        return carry

    lax.fori_loop(0, nch, zero, 0)

    def chunk(i, carry):
        steps = [(j, d) for j in range(HG_GROUP) for d in range(2)]
        rows = [pl.ds(pl.multiple_of(_chunk_index(i * HG_GROUP + j, d, nch, ncc) * C, C), C) for j, d in steps]
        q, k, v, logf = [], [], [], []
        for (j, d), rw in zip(steps, rows):
            pf = f_refs[d][0, rw, :]
            e = jnp.exp(-jnp.abs(pf))
            r = 1.0 / (1.0 + e)
            a = lb_ref[d, 0:1, :]
            b = lb_ref[d, 1:2, :] + (jnp.minimum(pf, 0.0) + jnp.log(r))
            logf.append(jnp.maximum(a, b) + jnp.log(1.0 + jnp.exp(-jnp.abs(a - b))))
            k.append(lb_ref[d, 2:3, :] * jnp.where(pf >= 0.0, e * r, r))
            q.append(_silu(q_ref[0, rw, :]))
            v.append(v_ref[0, rw, :])
        inst = [(n, d, h, slice(h * HEAD_DIM, (h + 1) * HEAD_DIM)) for n, (_, d) in enumerate(steps) for h in range(hb)]
        qh = [q[n][:, sl] for n, _, _, sl in inst]
        kh = [k[n][:, sl] for n, _, _, sl in inst]
        vh = [v[n][:, sl] for n, _, _, sl in inst]
        vb = [x.astype(BF16) for x in vh]

        def run(single_level):
            first = n_lev * C if single_level else 0
            seg = [_dot(cm_ref[d, first:, :], jnp.concatenate(_split_bf16(lf, 2), axis=0)) for lf, (_, d) in zip(logf, steps)]
            sg = [seg[n][:, sl] for n, _, _, sl in inst]
            cum = [x[-2 * C:-C] for x in sg]
            tot = [x[-C:] for x in sg]
            qe = [(a * jnp.exp(c)).astype(BF16) for a, c in zip(qh, cum)]
            if single_level:
                att = []
                for (_, d, _, _), a, b, c in zip(inst, qh, kh, cum):
                    mid = c[C // 2 - 1 + d:C // 2 + d, :]
                    pairs = _dot_nt((a * jnp.exp(c - mid)).astype(BF16), (b * jnp.exp(mid - c)).astype(BF16))
                    att.append(jnp.where(mk_ref[d, n_lev] > 0.5, pairs, 0.0))
            else:
                pair = []
                for a, b, x in zip(qh, kh, sg):
                    a, b = a.astype(BF16), b.astype(BF16)
                    e = [jnp.exp(x[lev * C:(lev + 1) * C]).astype(BF16) for lev in range(n_lev)]
                    pair.append([_dot_nt(a * ee, b * ee) for ee in e])
                att = [functools.reduce(jnp.add, [mk_ref[d, lev] * p[lev] for lev in range(n_lev)])
                       for (_, d, _, _), p in zip(inst, pair)]
            upd = [_dot_tn(x, (b * jnp.exp(t - c)).astype(BF16)) for x, b, t, c in zip(vb, kh, tot, cum)]
            st, cur = [], {}
            for m, (n, d, h, _) in enumerate(inst):
                s = cur[d, h] if (d, h) in cur else st_ref[d, h]
                st.append(s)
                cur[d, h] = s * jnp.exp(tot[m][0:1, :]) + upd[m]
            o_inter = [_dot_nt(a, s.astype(BF16)) for a, s in zip(qe, st)]
            o_intra = [_dot(a.astype(BF16), x) for a, x in zip(att, vb)]
            for m, (n, d, h, sl) in enumerate(inst):
                o_acc[rows[n], sl] += o_intra[m] + o_inter[m] + jnp.sum(qh[m] * kh[m], axis=-1, keepdims=True) * vh[m]
            for (d, h), s in cur.items():
                st_ref[d, h] = s

        halves = [jnp.sum(lf[n * (C // 2):(n + 1) * (C // 2)], axis=0, keepdims=True) for lf in logf for n in range(2)]
        safe = functools.reduce(jnp.minimum, [jnp.min(x) for x in halves]) > SAFE_LOG_DECAY
        pl.when(safe)(functools.partial(run, True))
        pl.when(jnp.logical_not(safe))(functools.partial(run, False))
        return carry

    lax.fori_loop(0, nch // HG_GROUP, chunk, 0)

    _scan_readout(o_acc, gt_ref, ng_ref, y_ref, hb)


def _hgrn2_scan(p, lb, norm_g, nctx):
    B, S, W5 = p.shape
    W = W5 // 5
    hb = 2 if W % (2 * HEAD_DIM) == 0 else 1
    wb = hb * HEAD_DIM
    nhg = W // wb
    C = SCAN_CHUNK
    tabs = [_level_tables(C, rev) for rev in (False, True)]
    cm = jnp.asarray(np.stack([np.concatenate([t[0], t[0]], axis=1) for t in tabs]), BF16)
    mk = jnp.asarray(np.stack([np.concatenate([t[1], t[1].sum(0, keepdims=True)]) for t in tabs]), F32)
    lbp = jnp.stack([jnp.log(lb), jnp.log1p(-lb), 1.0 - lb], axis=1)
    est = 2 * (5 * S * wb * 4 + S * wb * 2) + S * wb * 4 + 16 * C * wb * 4 * 8
    col = lambda k: pl.BlockSpec((1, S, wb), lambda b, g, k=k: (b, 0, k * nhg + g))
    return pl.pallas_call(
        functools.partial(_hgrn2_kernel, S=S, nctx=nctx, hb=hb),
        grid=(B, nhg),
        in_specs=[
            col(0), col(1), col(2), col(3), col(4),
            pl.BlockSpec((2, 3, wb), lambda b, g: (0, 0, g)),
            pl.BlockSpec((1, HEAD_DIM), lambda b, g: (0, 0)),
            pl.BlockSpec(cm.shape, lambda b, g: (0, 0, 0)),
            pl.BlockSpec(mk.shape, lambda b, g: (0, 0, 0, 0)),
        ],
        out_specs=pl.BlockSpec((1, S, wb), lambda b, g: (b, 0, g)),
        out_shape=jax.ShapeDtypeStruct((B, S, W), BF16),
        scratch_shapes=[pltpu.VMEM((S, wb), F32), pltpu.VMEM((2, hb, HEAD_DIM, HEAD_DIM), F32)],
        compiler_params=_params(("arbitrary", "arbitrary"), est),
        name="hgrn2_scan",
    )(p, p, p, p, p, lbp, norm_g.reshape(1, HEAD_DIM), cm, mk)


GDN_ROWS = 256


def _softplus(x):
    return jnp.maximum(x, 0.0) + jnp.log1p(jnp.exp(-jnp.abs(x)))


def _conv5_silu(x_ref, w_ref, r0, S, nctx):
    n = GDN_ROWS + 16
    lo = pl.multiple_of(jnp.maximum(r0 - 8, 0), 8)
    hi = pl.multiple_of(jnp.minimum(r0 + GDN_ROWS, S - 8), 8)
    keep_prev = jnp.where(jnp.logical_or(r0 == 0, r0 == nctx), 0.0, 1.0)
    keep_next = jnp.where(jnp.logical_or(r0 + GDN_ROWS == nctx, r0 + GDN_ROWS == S), 0.0, 1.0)
    ext = jnp.concatenate([x_ref[0, pl.ds(lo, 8), :] * keep_prev, x_ref[0, pl.ds(r0, GDN_ROWS), :],
                           x_ref[0, pl.ds(hi, 8), :] * keep_next], axis=0)
    half = GDN_CONV // 2
    acc = w_ref[half:half + 1, :] * ext[8:8 + GDN_ROWS]
    for off in range(-half, half + 1):
        if off != 0:
            acc = acc + w_ref[off + half:off + half + 1, :] * pltpu.roll(ext, (-off) % n, 0)[8:8 + GDN_ROWS]
    return _silu(acc)


def _l2norm(x):
    return x * lax.rsqrt(jnp.sum(x * x, axis=-1, keepdims=True) + RMS_EPS)


GDN_GROUP = 4


def _gdn_kernel(pq_ref, pk_ref, pv_ref, pz_ref, bag_ref, bat_ref, cq_ref, ck_ref, cv_ref, gl_ref, gc_ref, ng_ref,
                cmc_ref, cmr_ref, mk_ref, y_ref, qn_ref, kn_ref, vv_ref, o_acc, st_ref, lhs_s, ku_s, au_s, dc_s, *, S, nctx):
    C, G = SCAN_CHUNK, GDN_GROUP
    nch, ncc = S // C, nctx // C
    ngrp, ncg = nch // G, ncc // G
    n_lev = mk_ref.shape[1] - 2

    def prologue(i, carry):
        r0 = pl.multiple_of(i * GDN_ROWS, GDN_ROWS)
        rows = pl.ds(r0, GDN_ROWS)
        qn_ref[rows, :] = _l2norm(_conv5_silu(pq_ref, cq_ref, r0, S, nctx)) * (HEAD_DIM ** -0.5)
        kn_ref[rows, :] = _l2norm(_conv5_silu(pk_ref, ck_ref, r0, S, nctx))
        vv_ref[rows, :] = _conv5_silu(pv_ref, cv_ref, r0, S, nctx)
        o_acc[rows, :] = jnp.zeros((GDN_ROWS, 2 * HEAD_DIM), F32)
        return carry

    lax.fori_loop(0, S // GDN_ROWS, prologue, 0)
    st_ref[...] = jnp.zeros(st_ref.shape, F32)

    def group_terms(chunks, between):
        rows = [pl.ds(pl.multiple_of(c * C, C), C) for c, _ in chunks]
        ba = [bag_ref[0, 0, r, :] for r in rows]
        beta8 = [jax.nn.sigmoid(x) for x in ba]
        g8 = [gl_ref[0, 0:1, :] * _softplus(x + gl_ref[0, 1:2, :]) for x in ba]
        g8r = [gc_ref[0, :, 0:1] * _softplus(bat_ref[0, 0, c] + gc_ref[0, :, 1:2]) for c, _ in chunks]
        cum8 = [_dot(cmc_ref[d], jnp.concatenate(_split_bf16(x, 3), axis=0)) for x, (_, d) in zip(g8, chunks)]
        cum8r = [_dot(jnp.concatenate(_split_bf16(x, 3), axis=1), cmr_ref[d]) for x, (_, d) in zip(g8r, chunks)]
        kc = [kn_ref[r, :] for r in rows]
        qc = [qn_ref[r, :] for r in rows]
        kb = [x.astype(BF16) for x in kc]
        kk = [_dot_nt(x, x) for x in kb]
        qk = [_dot_nt(x.astype(BF16), y) for x, y in zip(qc, kb)]
        inst = [(i, h) for i in range(len(chunks)) for h in range(2)]
        beta, gcum, gtot, dec_i, lmat, tinv = [], [], [], [], [], []
        for i, h in inst:
            d = chunks[i][1]
            lb, la = 4 * d + h, 4 * d + 2 + h
            strict, incl = mk_ref[d, n_lev], mk_ref[d, n_lev + 1]
            beta.append(beta8[i][:, lb:lb + 1])
            gcum.append(cum8[i][:C, la:la + 1])
            gtot.append(cum8[i][C:, la:la + 1])
            diff = gcum[-1] - cum8r[i][la:la + 1, :]
            dec_i.append(jnp.exp(jnp.where(incl > 0.5, diff, NEG_BIG)))
            lmat.append(beta[-1] * kk[i] * jnp.exp(jnp.where(strict > 0.5, diff, NEG_BIG)))
            tinv.append((incl - strict) - lmat[-1] * mk_ref[d, 0])
        for lev in range(1, n_lev):
            tb = [t.astype(BF16) for t in tinv]
            x = [_dot(t, (lm * mk_ref[chunks[i][1], lev]).astype(BF16)).astype(BF16) for t, lm, (i, _) in zip(tb, lmat, inst)]
            tinv = [t - _dot(xi, ti) for t, xi, ti in zip(tinv, x, tb)]
            if lev <= len(between):
                between[lev - 1]()
        rhs = [jnp.concatenate([vv_ref[rows[i], h * HEAD_DIM:(h + 1) * HEAD_DIM] * b, kc[i] * (b * jnp.exp(g))], axis=1)
               for (i, h), b, g in zip(inst, beta, gcum)]
        sol = [_dot(t.astype(BF16), r.astype(BF16)).astype(BF16) for t, r in zip(tinv, rhs)]
        ku_kw = [_dot_tn((kc[i] * jnp.exp(gt - g)).astype(BF16), s) for (i, _), gt, g, s in zip(inst, gtot, gcum, sol)]
        au_aw = [_dot((qk[i] * dc).astype(BF16), s) for (i, _), dc, s in zip(inst, dec_i, sol)]
        for n, (i, h) in enumerate(inst):
            qt = qc[i] * jnp.exp(gcum[n]) - au_aw[n][:, HEAD_DIM:]
            lhs_s[2 * i + h] = jnp.concatenate([ku_kw[n][:, HEAD_DIM:], qt], axis=0).astype(BF16)
            ku_s[2 * i + h] = ku_kw[n][:, :HEAD_DIM]
            au_s[2 * i + h] = au_aw[n][:, :HEAD_DIM]
            dc_s[2 * i + h] = jnp.broadcast_to(jnp.exp(gtot[n][0:1, :]), (8, HEAD_DIM))

    chains = [(d, h) for d in range(2) for h in range(2)]

    def group_chunks(gi):
        base = [_chunk_index(gi, d, ngrp, ncg) * G for d in range(2)]
        return base, [(base[d] + j, d) for d in range(2) for j in range(G)]

    def state_step(gi, step):
        base, _ = group_chunks(gi)
        js = [step if d == 0 else G - 1 - step for d, _ in chains]
        slots = [(d * G + j) * 2 + h for j, (d, h) in zip(js, chains)]
        st = [st_ref[d, h] for d, h in chains]
        r = [_dot(lhs_s[n], s.astype(BF16)) for n, s in zip(slots, st)]
        for n, j, (d, h), s, rr in zip(slots, js, chains, st, r):
            rows = pl.ds(pl.multiple_of((base[d] + j) * C, C), C)
            o_acc[rows, h * HEAD_DIM:(h + 1) * HEAD_DIM] += rr[HEAD_DIM:] + au_s[n]
            st_ref[d, h] = dc_s[n, 0:1, :] * s - rr[:HEAD_DIM] + ku_s[n]

    def group(gi, carry):
        group_terms(group_chunks(gi)[1], [functools.partial(state_step, gi - 1, step) for step in range(G)])
        return carry

    group_terms(group_chunks(0)[1], [])
    lax.fori_loop(1, ngrp, group, 0)
    for step in range(G):
        state_step(ngrp - 1, step)

    _scan_readout(o_acc, pz_ref, ng_ref, y_ref, 2)


def _gdn_scan(p, ba, conv_w, a_log, dt_bias, norm_g, nctx):
    B, S, W6 = p.shape
    W = W6 // 6
    nq = W // HEAD_DIM
    C, G = SCAN_CHUNK, GDN_GROUP
    assert ba.shape[2] == 8 * nq and S % GDN_ROWS == 0 and nctx % GDN_ROWS == 0 and S % (C * G) == 0 and nctx % (C * G) == 0
    n_lev = int(np.log2(C))
    tri, masks = [], []
    for rev in (False, True):
        mats, mk = _level_tables(C, rev)
        tri.append(mats[n_lev * C:])
        strict = mk.sum(0)
        masks.append(np.concatenate([mk, strict[None], (strict + np.eye(C, dtype=np.float32))[None]], axis=0))
    cmc = jnp.asarray(np.stack([np.concatenate([t, t, t], axis=1) for t in tri]), BF16)
    cmr = jnp.asarray(np.stack([np.concatenate([t[:C].T, t[:C].T, t[:C].T], axis=0) for t in tri]), BF16)
    mk = jnp.asarray(np.stack(masks), F32)

    def per_group(a):
        a = a.reshape(a.shape[:-1] + (nq, 2))
        return jnp.moveaxis(a, -2, 0).reshape((nq,) + a.shape[:-4] + (8,))

    bag = jnp.moveaxis(per_group(ba.reshape(B, S, 2, 2, 2 * nq)), 0, 1)
    bat = jnp.swapaxes(bag.reshape(B, nq, S // C, C, 8), 3, 4)
    zeros = jnp.zeros((2, 2 * nq), F32)
    neg_rate = per_group(jnp.stack([zeros, -jnp.exp(a_log.astype(F32))], axis=1))
    dtb = per_group(jnp.stack([zeros, dt_bias.astype(F32)], axis=1))
    gl = jnp.stack([neg_rate, dtb], axis=1)
    q128, q256 = W // HEAD_DIM, W // (2 * HEAD_DIM)
    est = (2 * (2 * S * 128 * 4 + 2 * S * 256 * 4 + S * 128 * 4 + (S // C) * 8 * 128 * 4 + S * 256 * 2)
           + 2 * S * 128 * 4 + 2 * S * 256 * 4 + 16 * (3 * 128 * 128 * 4))
    return pl.pallas_call(
        functools.partial(_gdn_kernel, S=S, nctx=nctx),
        grid=(B, nq),
        in_specs=[
            pl.BlockSpec((1, S, HEAD_DIM), lambda b, g: (b, 0, g)),
            pl.BlockSpec((1, S, HEAD_DIM), lambda b, g: (b, 0, q128 + g)),
            pl.BlockSpec((1, S, 2 * HEAD_DIM), lambda b, g: (b, 0, 2 * q256 + g)),
            pl.BlockSpec((1, S, 2 * HEAD_DIM), lambda b, g: (b, 0, 4 * q256 + g)),
            pl.BlockSpec((1, 1, S, 8), lambda b, g: (b, g, 0, 0)),
            pl.BlockSpec((1, 1, S // C, 8, C), lambda b, g: (b, g, 0, 0, 0)),
            pl.BlockSpec((GDN_CONV, HEAD_DIM), lambda b, g: (0, g)),
            pl.BlockSpec((GDN_CONV, HEAD_DIM), lambda b, g: (0, q128 + g)),
            pl.BlockSpec((GDN_CONV, 2 * HEAD_DIM), lambda b, g: (0, 2 * q256 + g)),
            pl.BlockSpec((1, 2, 8), lambda b, g: (g, 0, 0)),
            pl.BlockSpec((1, 8, 2), lambda b, g: (g, 0, 0)),
            pl.BlockSpec((1, HEAD_DIM), lambda b, g: (0, 0)),
            pl.BlockSpec(cmc.shape, lambda b, g: (0, 0, 0)),
            pl.BlockSpec(cmr.shape, lambda b, g: (0, 0, 0)),
            pl.BlockSpec(mk.shape, lambda b, g: (0, 0, 0, 0)),
        ],
        out_specs=pl.BlockSpec((1, S, 2 * HEAD_DIM), lambda b, g: (b, 0, g)),
        out_shape=jax.ShapeDtypeStruct((B, S, 2 * W), BF16),
        scratch_shapes=[pltpu.VMEM((S, HEAD_DIM), F32), pltpu.VMEM((S, HEAD_DIM), F32), pltpu.VMEM((S, 2 * HEAD_DIM), F32),
                        pltpu.VMEM((S, 2 * HEAD_DIM), F32), pltpu.VMEM((2, 2, HEAD_DIM, HEAD_DIM), F32),
                        pltpu.VMEM((4 * G, HEAD_DIM + C, HEAD_DIM), BF16), pltpu.VMEM((4 * G, HEAD_DIM, HEAD_DIM), F32),
                        pltpu.VMEM((4 * G, C, HEAD_DIM), F32), pltpu.VMEM((4 * G, 8, HEAD_DIM), F32)],
        compiler_params=_params(("arbitrary", "arbitrary"), est),
        name="gdn_scan",
    )(p, p, p, p, bag, bat, conv_w, conv_w, conv_w, gl, jnp.swapaxes(gl, 1, 2), norm_g.reshape(1, HEAD_DIM), cmc, cmr, mk)


NA_ROWS = 256
NA_PAIRS = NA_WIN_R // 2
NA_UNROLL = 4


def _na_bias_table(rpb):
    q = np.arange(GRID_W)[:, None]
    kc = np.arange(GRID_W)[None, :]
    c0 = np.clip(q - NA_WIN_C // 2, 0, GRID_W - NA_WIN_C)
    valid = (kc >= c0) & (kc < c0 + NA_WIN_C)
    idx = np.clip(kc - q + NA_WIN_C - 1, 0, 2 * NA_WIN_C - 2)
    t = jnp.where(valid, rpb.astype(F32)[:, :, idx], NEG_BIG)
    return jnp.concatenate([t[:, :-1], t[:, 1:]], axis=-1)


def _softmax_pv(scores, values):
    m = functools.reduce(jnp.maximum, [jnp.max(s, axis=-1, keepdims=True) for s in scores])
    ps = [jnp.exp(s - m) for s in scores]
    den = functools.reduce(jnp.add, [jnp.sum(p, axis=-1, keepdims=True) for p in ps])
    num = functools.reduce(jnp.add, [_dot(p.astype(BF16), v) for p, v in zip(ps, values)])
    return num / den


def _na_kernel(q_ref, k_ref, v_ref, qg_ref, kg_ref, gm_ref, hm_ref, t2_ref, o_ref, qn_ref, kn_ref, vb_ref,
               *, S, nctx, grid_rows):
    scale = NA_HEAD_DIM ** -0.5

    def head_rms(x):
        hi, lo = _split_bf16(x * x, 2)
        ms = _dot(hi, gm_ref[...]) + _dot(lo, gm_ref[...])
        return x * lax.rsqrt(ms + RMS_EPS)

    def prologue(i, carry):
        rows = pl.ds(pl.multiple_of(i * NA_ROWS, NA_ROWS), NA_ROWS)
        qn_ref[rows, :] = (head_rms(q_ref[0, rows, :]) * (qg_ref[...] * scale)).astype(BF16)
        kn_ref[rows, :] = (head_rms(k_ref[0, rows, :]) * kg_ref[...]).astype(BF16)
        vb_ref[rows, :] = v_ref[0, rows, :].astype(BF16)
        return carry

    lax.fori_loop(0, S // NA_ROWS, prologue, 0)

    def attend(blocks):
        q4 = [jnp.concatenate([qn_ref[rows, :] * hm_ref[j].astype(BF16) for j in range(NA_GROUP)], axis=0)
              for rows, _ in blocks]
        scores = [[_dot_nt(q, k) if bias is None else _dot_nt(q, k) + bias for k, _, bias in parts]
                  for q, (_, parts) in zip(q4, blocks)]
        for (rows, parts), sc in zip(blocks, scores):
            o4 = _softmax_pv(sc, [v for _, v, _ in parts])
            out = functools.reduce(jnp.add, [o4[j * GRID_W:(j + 1) * GRID_W] * hm_ref[j] for j in range(NA_GROUP)])
            o_ref[0, rows, :] = out.astype(BF16)

    k_ctx, v_ctx = kn_ref[0:nctx, :], vb_ref[0:nctx, :]
    attend([(pl.ds(i * GRID_W, GRID_W), [(k_ctx, v_ctx, None)]) for i in range(nctx // GRID_W)])

    def grid_rows_step(i, carry):
        blocks = []
        for r in (NA_UNROLL * i + u for u in range(NA_UNROLL)):
            r0 = jnp.clip(r - NA_WIN_R // 2, 0, grid_rows - NA_WIN_R)
            dr0 = r0 - r + NA_WIN_R - 1
            rows = pl.ds(pl.multiple_of(nctx + r * GRID_W, GRID_W), GRID_W)
            keys = pl.ds(pl.multiple_of(nctx + r0 * GRID_W, GRID_W), NA_WIN_R * GRID_W)
            bias = jnp.concatenate([t2_ref[0, dr0 + 2 * m] for m in range(NA_PAIRS)], axis=1)
            blocks.append((rows, [(kn_ref[keys, :], vb_ref[keys, :], bias), (kn_ref[0:nctx, :], vb_ref[0:nctx, :], None)]))
        attend(blocks)
        return carry

    lax.fori_loop(0, grid_rows // NA_UNROLL, grid_rows_step, 0)


def _na_attention(p, q_gain, k_gain, rpb, nctx):
    B, S, D3 = p.shape
    D = D3 // 3
    grid_rows = (S - nctx) // GRID_W
    assert (S - nctx) % GRID_W == 0 and grid_rows >= NA_WIN_R and S % NA_ROWS == 0 and nctx % GRID_W == 0
    ng = D // HEAD_DIM
    assert grid_rows % NA_UNROLL == 0
    t2 = _na_bias_table(rpb).reshape(ng, NA_GROUP, 2 * NA_WIN_R - 2, GRID_W, 2 * GRID_W)
    t2 = jnp.swapaxes(t2, 1, 2).reshape(ng, 2 * NA_WIN_R - 2, NA_GROUP * GRID_W, 2 * GRID_W)
    lane_head = np.arange(HEAD_DIM) // NA_HEAD_DIM
    gm = jnp.asarray((lane_head[:, None] == lane_head[None, :]) / NA_HEAD_DIM, BF16)
    hm = jnp.asarray((np.arange(NA_GROUP)[:, None, None] == lane_head[None, None, :]), F32)
    tile = lambda g: jnp.tile(g.astype(F32), NA_GROUP).reshape(1, HEAD_DIM)
    est = (2 * (3 * S * 128 * 4 + t2.shape[1] * t2.shape[2] * 128 * 4 + S * 128 * 2) + 3 * S * 128 * 2
           + 6 * NA_UNROLL * NA_GROUP * GRID_W * (NA_WIN_R * GRID_W + nctx) * 4)
    col = lambda k: pl.BlockSpec((1, S, HEAD_DIM), lambda b, g, k=k: (b, 0, k * ng + g))
    return pl.pallas_call(
        functools.partial(_na_kernel, S=S, nctx=nctx, grid_rows=grid_rows),
        grid=(B, ng),
        in_specs=[
            col(0), col(1), col(2),
            pl.BlockSpec((1, HEAD_DIM), lambda b, g: (0, 0)),
            pl.BlockSpec((1, HEAD_DIM), lambda b, g: (0, 0)),
            pl.BlockSpec(gm.shape, lambda b, g: (0, 0)),
            pl.BlockSpec(hm.shape, lambda b, g: (0, 0, 0)),
            pl.BlockSpec((1,) + t2.shape[1:], lambda b, g: (g, 0, 0, 0)),
        ],
        out_specs=pl.BlockSpec((1, S, HEAD_DIM), lambda b, g: (b, 0, g)),
        out_shape=jax.ShapeDtypeStruct((B, S, D), BF16),
        scratch_shapes=[pltpu.VMEM((S, HEAD_DIM), BF16)] * 3,
        compiler_params=_params(("arbitrary", "arbitrary"), est),
        name="na_attention",
    )(p, p, p, tile(q_gain), tile(k_gain), gm, hm, t2)


def _hgrn2_lower_bound(lb_logits, j):
    cs = jnp.cumsum(jax.nn.softmax(lb_logits.astype(F32), axis=0), axis=0)
    return cs[j] - cs[0]


def kernel(x, c, ctx, c_ctx, ada_w, ada_b, norm_mix_g, norm_ffn_g, hg_w_in, hg_lb_logits, hg_norm_g, hg_w_out,
           gdn_w_in, gdn_conv_w, gdn_a_log, gdn_dt_bias, gdn_norm_g, gdn_w_out, na_w_qkv, na_q_norm_g, na_k_norm_g,
           na_rpb, na_w_out, ffn_w_up, ffn_conv_w, ffn_w_down):
    depth = ada_w.shape[0]
    nctx = ctx.shape[1]
    D = x.shape[2]
    xs = jnp.concatenate([ctx, x], axis=1)
    mods = _ada_params(c, c_ctx, ada_w, ada_b)
    for i in range(depth):
        m, j = i % 3, i // 3
        mod = mods[i]
        if m == 0:
            p = _mod_linear(xs, mod, norm_mix_g[i], hg_w_in[j].astype(BF16), nctx, 0, 1, F32)
            y = _hgrn2_scan(p, _hgrn2_lower_bound(hg_lb_logits, j), hg_norm_g[j], nctx)
            w_out = hg_w_out[j]
        elif m == 1:
            w_in = gdn_w_in[j].astype(BF16)
            p = _mod_linear(xs, mod, norm_mix_g[i], w_in[:, :6 * D], nctx, 0, 1, F32)
            ba = _mod_linear(xs, mod, norm_mix_g[i], w_in[:, 6 * D:], nctx, 0, 1, F32)
            y = _gdn_scan(p, ba, gdn_conv_w[j], gdn_a_log[j], gdn_dt_bias[j], gdn_norm_g[j], nctx)
            w_out = gdn_w_out[j]
        else:
            p = _mod_linear(xs, mod, norm_mix_g[i], na_w_qkv[j].astype(BF16), nctx, 0, 1, F32)
            y = _na_attention(p, na_q_norm_g[j], na_k_norm_g[j], na_rpb[j], nctx)
            w_out = na_w_out[j]
        xs = _out_linear(y, w_out.astype(BF16), xs, mod, nctx, 2)
        w_up, conv_w, w_down = _ffn_weights(ffn_w_up[i], ffn_conv_w[i], ffn_w_down[i])
        xs = _conv_ffn(xs, mod, norm_ffn_g[i], w_up, conv_w, w_down, nctx)
    return xs[:, nctx:]
```
